```python
import math
import jax, jax.numpy as jnp
from jax import lax
import numpy as np

D_MODEL = 2048
BATCH = 32
SEQ = 256
DEPTH = 4
DEC_BATCH = 8
DEC_SEQ = 2048
PAST_LEN = 512

GRID_W = 64
N_MIXERS = 3
EPS = 1e-6
HY_SHORT_W = 3
HY_BANDS = 16
HY_EMB = 1 + 2 * HY_BANDS
HY_FFN = 64
HY_SIN_FREQ = 1.0
HY_DECAY_MIN = 3.07
HY_DECAY_MAX = 15.35
SSD_DI = 2 * D_MODEL
SSD_P = 64
SSD_H = SSD_DI // SSD_P
SSD_G = 8
SSD_R = SSD_H // SSD_G
SSD_N = 128
SSD_CONV_W = 3
SSD_CHUNK = 128
SSD_CONV_DIM = SSD_DI + 2 * SSD_G * SSD_N
SSD_IN_DIM = SSD_DI + SSD_CONV_DIM + 2 * SSD_H
MLA_H = 16
MLA_Q_RANK = 512
MLA_KV_RANK = 512
MLA_DN = 128
MLA_DR = 64
MLA_DV = 128
ROPE_BASE = 10000.0
ROPE_AXIS_DIM = MLA_DR // 2
Q_BLOCK = 128
D_FF = 5632
N_EXPERTS = 8
TOP_K = 2
D_EXPERT = 7168
MOE_BLOCK = 256
N_HY = (DEPTH + 2) // 3
N_SSD = (DEPTH + 1) // 3
N_MLA = DEPTH // 3
N_DENSE = (DEPTH + 1) // 2
N_MOE = DEPTH // 2

kernel_name = "hybrid_hyena_ssd_mla_diffusion_step"


def rmsnorm(x, g):
    xf = x.astype(jnp.float32)
    y = xf * lax.rsqrt(jnp.mean(xf * xf, axis=-1, keepdims=True) + EPS)
    return (y * g.astype(jnp.float32)).astype(x.dtype)


def dwconv_centred(x, w, b):
    k = w.shape[0]
    y = lax.conv_general_dilated(x, w[:, None, :].astype(x.dtype), window_strides=(1,),
                                 padding=[(k // 2, k // 2)],
                                 dimension_numbers=("NWC", "WIO", "NWC"),
                                 feature_group_count=x.shape[-1])
    return y + b.astype(x.dtype)


def swiglu(h, w_gate, w_up, w_down):
    return (jax.nn.silu(h @ w_gate) * (h @ w_up)) @ w_down


def moe_swiglu(h, w_router, b_router, w_gate, w_up, w_down):
    b, L, d = h.shape
    T = b * L
    x2 = h.reshape(T, d)
    logits = (x2 @ w_router).astype(jnp.float32) + b_router.astype(jnp.float32)
    top_logit, top_e = lax.top_k(logits, TOP_K)
    gates = jax.nn.softmax(top_logit, axis=-1)
    flat_e = top_e.reshape(-1)
    flat_tok = jnp.arange(T * TOP_K, dtype=jnp.int32) // TOP_K
    order = jnp.argsort(flat_e)
    sorted_e = flat_e[order]
    counts = jnp.bincount(flat_e, length=N_EXPERTS)
    grp_start = jnp.cumsum(counts) - counts
    padded = (counts + MOE_BLOCK - 1) // MOE_BLOCK * MOE_BLOCK
    pad_end = jnp.cumsum(padded)
    pad_start = pad_end - padded
    dest = pad_start[sorted_e] + jnp.arange(T * TOP_K, dtype=jnp.int32) - grp_start[sorted_e]
    n_blocks = -(-(T * TOP_K) // MOE_BLOCK) + N_EXPERTS
    n_rows = n_blocks * MOE_BLOCK
    row_tok = jnp.full((n_rows,), T, jnp.int32).at[dest].set(flat_tok[order])
    row_w = jnp.zeros((n_rows,), jnp.float32).at[dest].set(gates.reshape(-1)[order])
    blk_start = jnp.arange(n_blocks, dtype=jnp.int32) * MOE_BLOCK
    blk_e = jnp.minimum(jnp.searchsorted(pad_end, blk_start, side="right"), N_EXPERTS - 1)
    x_pad = jnp.concatenate([x2, jnp.zeros((1, d), x2.dtype)], axis=0)

    def expert_block(args):
        tok, e = args
        xb = x_pad[tok]
        hb = jax.nn.silu(xb @ w_gate[e]) * (xb @ w_up[e])
        return hb @ w_down[e]

    y_rows = lax.map(expert_block, (row_tok.reshape(n_blocks, MOE_BLOCK), blk_e))
    y = jnp.zeros((T + 1, d), jnp.float32).at[row_tok].add(
        y_rows.reshape(n_rows, d).astype(jnp.float32) * row_w[:, None])
    return y[:T].reshape(b, L, d).astype(h.dtype)


def hyena_filters(L, w1, b1, w2, b2, w3, decay):
    t = jnp.arange(L, dtype=jnp.float32)
    t01 = t / (L - 1)
    bands = jnp.linspace(1e-4, HY_BANDS - 1, HY_BANDS, dtype=jnp.float32)
    ang = (2.0 * math.pi / L) * t[:, None] * bands[None, :]
    z = jnp.concatenate([t01[:, None], jnp.cos(ang), jnp.sin(ang)], axis=-1)
    f = jnp.sin(HY_SIN_FREQ * (z @ w1.astype(jnp.float32) + b1.astype(jnp.float32)))
    f = jnp.sin(HY_SIN_FREQ * (f @ w2.astype(jnp.float32) + b2.astype(jnp.float32)))
    k = (f @ w3.astype(jnp.float32)).reshape(L, 2, D_MODEL)
    k = k * jnp.exp(-t01[:, None, None] * decay.astype(jnp.float32))
    k_f, k_b = k[:, 0], k[:, 1]
    l1 = jnp.sum(jnp.abs(k_f), axis=0) + jnp.sum(jnp.abs(k_b[1:]), axis=0)
    return k_f / l1, k_b / l1


def bidir_long_conv(u, k_f, k_b):
    L = u.shape[1]
    kc = jnp.concatenate([k_f, jnp.zeros((1, D_MODEL), jnp.float32), k_b[:0:-1]], axis=0)
    uf = jnp.fft.rfft(u.astype(jnp.float32), n=2 * L, axis=1)
    kf = jnp.fft.rfft(kc, n=2 * L, axis=0)
    return jnp.fft.irfft(uf * kf[None], n=2 * L, axis=1)[:, :L]


def hyena_mixer(h, w_in, conv_w, conv_b, f_w1, f_b1, f_w2, f_b2, f_w3, decay, skip, w_out):
    L = h.shape[1]
    u = dwconv_centred(h @ w_in, conv_w, conv_b)
    x0, x1, v = jnp.split(u, 3, axis=-1)
    k_f, k_b = hyena_filters(L, f_w1, f_b1, f_w2, f_b2, f_w3, decay)
    z = v * x1
    z = bidir_long_conv(z, k_f, k_b) + z.astype(jnp.float32) * skip.astype(jnp.float32)
    return (x0 * z).astype(h.dtype) @ w_out


def ssd_scan(x, dt, a_neg, B, C, init):
    b, L = x.shape[:2]
    nc = L // SSD_CHUNK
    xs = (x * dt[..., None]).reshape(b, nc, SSD_CHUNK, SSD_G, SSD_R, SSD_P)
    a = (dt * a_neg).reshape(b, nc, SSD_CHUNK, SSD_G, SSD_R)
    Bc = B.reshape(b, nc, SSD_CHUNK, SSD_G, SSD_N)
    Cc = C.reshape(b, nc, SSD_CHUNK, SSD_G, SSD_N)
    a_cum = jnp.cumsum(a, axis=2)
    causal = jnp.tril(jnp.ones((SSD_CHUNK, SSD_CHUNK), bool))[:, :, None, None]
    seg = a_cum[:, :, :, None] - a_cum[:, :, None]
    decay_in = jnp.exp(jnp.where(causal, seg, -jnp.inf))
    cb = jnp.einsum("bclgn,bcsgn->bclsg", Cc, Bc)
    y_diag = jnp.einsum("bclsg,bclsgr,bcsgrp->bclgrp", cb, decay_in, xs)
    decay_end = jnp.exp(a_cum[:, :, -1:] - a_cum)
    chunk_states = jnp.einsum("bclgn,bclgr,bclgrp->bcgrpn", Bc, decay_end, xs)
    chunk_decay = jnp.exp(a_cum[:, :, -1])

    def step(s, inp):
        st, dec = inp
        return s * dec[..., None, None] + st, s

    s0 = init.astype(jnp.float32).reshape(b, SSD_G, SSD_R, SSD_P, SSD_N)
    final, s_in = lax.scan(step, s0, (jnp.moveaxis(chunk_states, 1, 0), jnp.moveaxis(chunk_decay, 1, 0)))
    s_in = jnp.moveaxis(s_in, 0, 1)
    y_off = jnp.einsum("bclgn,bcgrpn,bclgr->bclgrp", Cc, s_in, jnp.exp(a_cum))
    y = (y_diag + y_off).reshape(b, L, SSD_H, SSD_P)
    return y, final.reshape(b, SSD_H, SSD_P, SSD_N)


def ssd_mixer(h, w_in, conv_w, conv_b, dt_bias, a_log, d_skip, norm_g, w_out, init_f, init_b):
    b, L, _ = h.shape
    proj = h @ w_in
    z = proj[..., :SSD_DI]
    xbc = jax.nn.silu(dwconv_centred(proj[..., SSD_DI:SSD_DI + SSD_CONV_DIM], conv_w, conv_b))
    dt_raw = proj[..., SSD_DI + SSD_CONV_DIM:].astype(jnp.float32).reshape(b, L, 2, SSD_H)
    dt = jax.nn.softplus(dt_raw + dt_bias.astype(jnp.float32))
    a_neg = -jnp.exp(a_log.astype(jnp.float32))
    x = xbc[..., :SSD_DI].reshape(b, L, SSD_H, SSD_P)
    Bm = xbc[..., SSD_DI:SSD_DI + SSD_G * SSD_N].reshape(b, L, SSD_G, SSD_N)
    Cm = xbc[..., SSD_DI + SSD_G * SSD_N:].reshape(b, L, SSD_G, SSD_N)
    y_f, s_f = ssd_scan(x, dt[:, :, 0], a_neg[0], Bm, Cm, init_f)
    y_b, s_b = ssd_scan(jnp.flip(x, 1), jnp.flip(dt[:, :, 1], 1), a_neg[1],
                        jnp.flip(Bm, 1), jnp.flip(Cm, 1), init_b)
    y = y_f + jnp.flip(y_b, 1) + x.astype(jnp.float32) * d_skip.astype(jnp.float32)[:, None]
    y = y.reshape(b, L, SSD_DI).astype(h.dtype)
    y = rmsnorm(y * jax.nn.silu(z), norm_g)
    return y @ w_out, s_f.astype(h.dtype), s_b.astype(h.dtype)


def axial_rope_tables(n_tokens):
    rows = n_tokens // GRID_W
    row = jnp.broadcast_to(jnp.arange(rows, dtype=jnp.float32)[:, None], (rows, GRID_W)).reshape(-1)
    col = jnp.broadcast_to(jnp.arange(GRID_W, dtype=jnp.float32)[None, :], (rows, GRID_W)).reshape(-1)
    inv_freq = ROPE_BASE ** (-jnp.arange(0, ROPE_AXIS_DIM, 2, dtype=jnp.float32) / ROPE_AXIS_DIM)
    ang = jnp.concatenate([row[:, None] * inv_freq, col[:, None] * inv_freq], axis=-1)
    return jnp.cos(ang), jnp.sin(ang)


def apply_rope(x, cos, sin):
    half = MLA_DR // 2
    x1 = x[..., :half].astype(jnp.float32)
    x2 = x[..., half:].astype(jnp.float32)
    return jnp.concatenate([x1 * cos - x2 * sin, x2 * cos + x1 * sin], axis=-1).astype(x.dtype)


def mla_qkv(h, w_dq, q_norm_g, w_uq, w_dkv, kv_norm_g):
    b, L, _ = h.shape
    q = (rmsnorm(h @ w_dq, q_norm_g) @ w_uq).reshape(b, L, MLA_H, MLA_DN + MLA_DR)
    kv_a = h @ w_dkv
    c_kv = rmsnorm(kv_a[..., :MLA_KV_RANK], kv_norm_g)
    return q[..., :MLA_DN], q[..., MLA_DN:], c_kv, kv_a[..., MLA_KV_RANK:]


def mla_attend(q_nope, q_pe, c_kv, k_pe, w_ukv, w_o):
    b, Lq = q_nope.shape[:2]
    Lk = c_kv.shape[1]
    kv = (c_kv @ w_ukv).reshape(b, Lk, MLA_H, MLA_DN + MLA_DV)
    k_nope, v = kv[..., :MLA_DN], kv[..., MLA_DN:]
    nb = Lq // Q_BLOCK
    scale = (MLA_DN + MLA_DR) ** -0.5

    def to_blocks(t):
        return jnp.moveaxis(t.reshape(b, nb, Q_BLOCK, *t.shape[2:]), 1, 0)

    def one_block(args):
        qn, qp = args
        s = jnp.einsum("bqhd,bkhd->bhqk", qn, k_nope) + jnp.einsum("bqhd,bkd->bhqk", qp, k_pe)
        p = jax.nn.softmax(s.astype(jnp.float32) * scale, axis=-1).astype(v.dtype)
        return jnp.einsum("bhqk,bkhd->bqhd", p, v)

    o = lax.map(one_block, (to_blocks(q_nope), to_blocks(q_pe)))
    o = jnp.moveaxis(o, 0, 1).reshape(b, Lq, MLA_H * MLA_DV)
    return o @ w_o


def setup_inputs(seed: int = 0) -> dict:
    key = jax.random.key(seed)
    ks = iter(jax.random.split(key, 64))
    D = D_MODEL

    def nrm(shape, scale=1.0):
        return scale * jax.random.normal(next(ks), shape, jnp.float32)

    def gain(shape):
        return 1.0 + nrm(shape, 0.02)

    def unif(shape, lo, hi):
        return jax.random.uniform(next(ks), shape, jnp.float32, lo, hi)

    inp = {}
    inp["x_prompt"] = nrm((BATCH, SEQ, D))
    inp["x_sample"] = nrm((DEC_BATCH, DEC_SEQ, D))
    inp["c"] = nrm((DEC_BATCH, D))
    inp["cache_mla_ckv"] = nrm((DEC_BATCH, N_MLA, PAST_LEN, MLA_KV_RANK))
    inp["cache_mla_kpe"] = nrm((DEC_BATCH, N_MLA, PAST_LEN, MLA_DR))
    inp["state_ssd_fwd"] = nrm((DEC_BATCH, N_SSD, SSD_H, SSD_P, SSD_N), 0.5)
    inp["state_ssd_bwd"] = nrm((DEC_BATCH, N_SSD, SSD_H, SSD_P, SSD_N), 0.5)
    inp["c_ctx"] = nrm((D,))
    inp["norm1_g"] = gain((DEPTH, D))
    inp["norm2_g"] = gain((DEPTH, D))
    inp["w_mod"] = nrm((DEPTH, D, 6 * D), D ** -0.5)
    inp["b_mod"] = nrm((DEPTH, 6 * D), 0.02)
    inp["norm_f_g"] = gain((D,))
    inp["hy_w_in"] = nrm((N_HY, D, 3 * D), D ** -0.5)
    inp["hy_conv_w"] = nrm((N_HY, HY_SHORT_W, 3 * D), HY_SHORT_W ** -0.5)
    inp["hy_conv_b"] = nrm((N_HY, 3 * D), 0.02)
    inp["hy_f_w1"] = nrm((N_HY, HY_EMB, HY_FFN), HY_EMB ** -0.5)
    inp["hy_f_b1"] = nrm((N_HY, HY_FFN), 0.02)
    inp["hy_f_w2"] = nrm((N_HY, HY_FFN, HY_FFN), HY_FFN ** -0.5)
    inp["hy_f_b2"] = nrm((N_HY, HY_FFN), 0.02)
    inp["hy_f_w3"] = nrm((N_HY, HY_FFN, 2 * D), HY_FFN ** -0.5)
    inp["hy_decay"] = unif((N_HY, 2, D), HY_DECAY_MIN, HY_DECAY_MAX)
    inp["hy_skip"] = nrm((N_HY, D))
    inp["hy_w_out"] = nrm((N_HY, D, D), D ** -0.5)
    inp["ssd_w_in"] = nrm((N_SSD, D, SSD_IN_DIM), D ** -0.5)
    inp["ssd_conv_w"] = nrm((N_SSD, SSD_CONV_W, SSD_CONV_DIM), SSD_CONV_W ** -0.5)
    inp["ssd_conv_b"] = nrm((N_SSD, SSD_CONV_DIM), 0.02)
    dt0 = jnp.exp(unif((N_SSD, 2, SSD_H), math.log(1e-3), math.log(1e-1)))
    inp["ssd_dt_bias"] = dt0 + jnp.log(-jnp.expm1(-dt0))
    inp["ssd_a_log"] = jnp.log(unif((N_SSD, 2, SSD_H), 1.0, 16.0))
    inp["ssd_d"] = 1.0 + nrm((N_SSD, SSD_H), 0.1)
    inp["ssd_norm_g"] = gain((N_SSD, SSD_DI))
    inp["ssd_w_out"] = nrm((N_SSD, SSD_DI, D), SSD_DI ** -0.5)
    inp["mla_w_dq"] = nrm((N_MLA, D, MLA_Q_RANK), D ** -0.5)
    inp["mla_q_norm_g"] = gain((N_MLA, MLA_Q_RANK))
    inp["mla_w_uq"] = nrm((N_MLA, MLA_Q_RANK, MLA_H * (MLA_DN + MLA_DR)), MLA_Q_RANK ** -0.5)
    inp["mla_w_dkv"] = nrm((N_MLA, D, MLA_KV_RANK + MLA_DR), D ** -0.5)
    inp["mla_kv_norm_g"] = gain((N_MLA, MLA_KV_RANK))
    inp["mla_w_ukv"] = nrm((N_MLA, MLA_KV_RANK, MLA_H * (MLA_DN + MLA_DV)), MLA_KV_RANK ** -0.5)
    inp["mla_w_o"] = nrm((N_MLA, MLA_H * MLA_DV, D), (MLA_H * MLA_DV) ** -0.5)
    inp["ffn_w_gate"] = nrm((N_DENSE, D, D_FF), D ** -0.5)
    inp["ffn_w_up"] = nrm((N_DENSE, D, D_FF), D ** -0.5)
    inp["ffn_w_down"] = nrm((N_DENSE, D_FF, D), D_FF ** -0.5)
    inp["moe_w_router"] = nrm((N_MOE, D, N_EXPERTS), D ** -0.5)
    inp["moe_b_router"] = nrm((N_MOE, N_EXPERTS), 0.01)
    inp["moe_w_gate"] = nrm((N_MOE, N_EXPERTS, D, D_EXPERT), D ** -0.5)
    inp["moe_w_up"] = nrm((N_MOE, N_EXPERTS, D, D_EXPERT), D ** -0.5)
    inp["moe_w_down"] = nrm((N_MOE, N_EXPERTS, D_EXPERT, D), D_EXPERT ** -0.5)
    return inp


def reference(x_prompt, x_sample, c, cache_mla_ckv, cache_mla_kpe, state_ssd_fwd, state_ssd_bwd, c_ctx,
              norm1_g, norm2_g, w_mod, b_mod, norm_f_g,
              hy_w_in, hy_conv_w, hy_conv_b, hy_f_w1, hy_f_b1, hy_f_w2, hy_f_b2, hy_f_w3, hy_decay,
              hy_skip, hy_w_out,
              ssd_w_in, ssd_conv_w, ssd_conv_b, ssd_dt_bias, ssd_a_log, ssd_d, ssd_norm_g, ssd_w_out,
              mla_w_dq, mla_q_norm_g, mla_w_uq, mla_w_dkv, mla_kv_norm_g, mla_w_ukv, mla_w_o,
              ffn_w_gate, ffn_w_up, ffn_w_down,
              moe_w_router, moe_b_router, moe_w_gate, moe_w_up, moe_w_down):

    def modulation(i, cond):
        m = jax.nn.silu(cond) @ w_mod[i] + b_mod[i]
        return jnp.split(m[:, None, :], 6, axis=-1)

    def pre(x, g, shift, scale):
        return rmsnorm(x, g) * (1.0 + scale) + shift

    def hyena(j, h):
        return hyena_mixer(h, hy_w_in[j], hy_conv_w[j], hy_conv_b[j], hy_f_w1[j], hy_f_b1[j], hy_f_w2[j],
                           hy_f_b2[j], hy_f_w3[j], hy_decay[j], hy_skip[j], hy_w_out[j])

    def ssd(j, h, s_f, s_b):
        return ssd_mixer(h, ssd_w_in[j], ssd_conv_w[j], ssd_conv_b[j], ssd_dt_bias[j], ssd_a_log[j],
                         ssd_d[j], ssd_norm_g[j], ssd_w_out[j], s_f, s_b)

    def mla_proj(j, h):
        return mla_qkv(h, mla_w_dq[j], mla_q_norm_g[j], mla_w_uq[j], mla_w_dkv[j], mla_kv_norm_g[j])

    def channel(i, h):
        j = i // 2
        if i % 2 == 0:
            return swiglu(h, ffn_w_gate[j], ffn_w_up[j], ffn_w_down[j])
        return moe_swiglu(h, moe_w_router[j], moe_b_router[j], moe_w_gate[j], moe_w_up[j], moe_w_down[j])

    xp = x_prompt
    ckv_new, kpe_new, sf_new, sb_new = [], [], [], []
    for i in range(DEPTH):
        sh1, sc1, g1, sh2, sc2, g2 = modulation(i, c_ctx[None, :])
        h = pre(xp, norm1_g[i], sh1, sc1)
        j = i // N_MIXERS
        kind = i % N_MIXERS
        if kind == 0:
            out = hyena(j, h)
        elif kind == 1:
            zero_state = jnp.zeros((h.shape[0], SSD_H, SSD_P, SSD_N), h.dtype)
            out, s_f, s_b = ssd(j, h, zero_state, zero_state)
            sf_new.append(s_f)
            sb_new.append(s_b)
        else:
            qn, qp, ckv, kpe = mla_proj(j, h)
            out = mla_attend(qn, qp, ckv, kpe, mla_w_ukv[j], mla_w_o[j])
            ckv_new.append(ckv)
            kpe_new.append(kpe)
        xp = xp + g1 * out
        h = pre(xp, norm2_g[i], sh2, sc2)
        xp = xp + g2 * channel(i, h)
    y_prompt = rmsnorm(xp, norm_f_g)

    xl = x_sample
    cos, sin = axial_rope_tables(xl.shape[1])
    for i in range(DEPTH):
        sh1, sc1, g1, sh2, sc2, g2 = modulation(i, c)
        h = pre(xl, norm1_g[i], sh1, sc1)
        j = i // N_MIXERS
        kind = i % N_MIXERS
        if kind == 0:
            out = hyena(j, h)
        elif kind == 1:
            out, _, _ = ssd(j, h, state_ssd_fwd[:, j], state_ssd_bwd[:, j])
        else:
            qn, qp, ckv, kpe = mla_proj(j, h)
            qp = apply_rope(qp, cos[:, None, :], sin[:, None, :])
            kpe = apply_rope(kpe, cos, sin)
            ckv_all = jnp.concatenate([cache_mla_ckv[:, j], ckv], axis=1)
            kpe_all = jnp.concatenate([cache_mla_kpe[:, j], kpe], axis=1)
            out = mla_attend(qn, qp, ckv_all, kpe_all, mla_w_ukv[j], mla_w_o[j])
        xl = xl + g1 * out
        h = pre(xl, norm2_g[i], sh2, sc2)
        xl = xl + g2 * channel(i, h)
    y_sample = rmsnorm(xl, norm_f_g)

    new_mla_ckv = jnp.stack(ckv_new, axis=1)
    new_mla_kpe = jnp.stack(kpe_new, axis=1)
    new_ssd_fwd = jnp.stack(sf_new, axis=1)
    new_ssd_bwd = jnp.stack(sb_new, axis=1)
    return (y_prompt, y_sample, new_mla_ckv, new_mla_kpe, new_ssd_fwd, new_ssd_bwd)
```

```python
import functools
import math

import jax
import jax.numpy as jnp
from jax import lax
from jax.experimental import pallas as pl
from jax.experimental.pallas import tpu as pltpu

F32 = jnp.float32
BF16 = jnp.bfloat16
EPS = 1e-6
N_MIXERS = 3
MLA_HEADS = 16
GRID_W = 64
ROPE_BASE = 10000.0
HY_BANDS = 16
HY_SIN_FREQ = 1.0
SSD_CHUNK = 128
LANES = 128
VMEM_LIMIT_BYTES = 56 * 1024 * 1024
HIGHEST = lax.Precision.HIGHEST


def _cparams(n_axes):
    return pltpu.CompilerParams(dimension_semantics=("arbitrary",) * n_axes,
                                vmem_limit_bytes=VMEM_LIMIT_BYTES)


def _pick(n, candidates):
    for c in candidates:
        if n % c == 0:
            return c
    raise ValueError(f"no tile for {n} in {candidates}")


def _silu(x):
    return x * (1.0 / (1.0 + jnp.exp(-x)))


def _dot(a, b):
    return jnp.dot(a, b, preferred_element_type=F32)


def _dot_nt(a, b):
    return lax.dot_general(a, b, (((1,), (1,)), ((), ())), preferred_element_type=F32)


def _dot_tn(a, b):
    return lax.dot_general(a, b, (((0,), (0,)), ((), ())), preferred_element_type=F32)


class _Tokens:
    def __init__(self, bp, lp, bs, ls):
        self.bp, self.lp, self.bs, self.ls = bp, lp, bs, ls
        self.tp, self.ts = bp * lp, bs * ls
        self.t = self.tp + self.ts
        self.groups = 1 + bs
        assert self.tp % ls == 0 or self.tp % lp == 0

    def row_tile(self, cap=1024):
        return _pick(math.gcd(self.tp, self.ls), [c for c in (1024, 512, 256, 128, 64, 32, 16, 8) if c <= cap])

    def group_of(self, row_start):
        return jnp.where(row_start < self.tp, 0, 1 + (row_start - self.tp) // self.ls)


def _mm_body(*refs, pre, swiglu, epi, nk, nj, grouped, slab_in, slab_out, precise, tn):
    refs = list(refs)
    if grouped:
        refs.pop(0)
        refs.pop(0)
    x_ref = refs.pop(0)
    g_ref = refs.pop(0) if pre else None
    sh_ref = sc_ref = None
    if pre == "normmod":
        sh_ref = refs.pop(0)
        sc_ref = refs.pop(0)
    w_ref = refs.pop(0)
    w2_ref = refs.pop(0) if swiglu else None
    res_ref = gate_ref = None
    if epi == "resgate":
        res_ref = refs.pop(0)
        gate_ref = refs.pop(0)
    o_ref = refs.pop(0)
    h_scr = refs.pop(0) if (pre or slab_in) else None
    acc_ref = refs.pop(0) if nk > 1 else None
    j = pl.program_id(1)
    k = pl.program_id(2)

    if slab_in:
        @pl.when(j == 0)
        def _():
            tm, kdim = h_scr.shape
            slab = kdim // LANES
            for s in range(slab):
                h_scr[:, s * LANES:(s + 1) * LANES] = x_ref[pl.ds(s, tm, stride=slab), :].astype(BF16)

    if pre:
        @pl.when(j == 0)
        def _():
            xf = x_ref[...].astype(F32)
            ms = jnp.mean(xf * xf, axis=-1, keepdims=True)
            y = (xf * lax.rsqrt(ms + EPS)) * g_ref[...]
            if pre == "normmod":
                y = y * (1.0 + sc_ref[0]) + sh_ref[0]
            h_scr[...] = y.astype(BF16)

    def product(wr):
        if precise:
            return jnp.dot(x_ref[...], wr[...], preferred_element_type=F32, precision=HIGHEST)
        xb = h_scr[...] if (pre or slab_in) else x_ref[...].astype(BF16)
        return _dot(xb, wr[...].astype(BF16))

    def finish(a, a2):
        if swiglu:
            a = _silu(a) * a2
        if epi == "resgate":
            a = res_ref[...] + gate_ref[0] * a
        if slab_out:
            assert nj == 1
            slab = tn // LANES
            tm = o_ref.shape[0] // slab
            for s in range(slab):
                o_ref[pl.ds(s, tm, stride=slab), :] = a[:, s * LANES:(s + 1) * LANES].astype(o_ref.dtype)
        else:
            o_ref[...] = a.astype(o_ref.dtype)

    a = product(w_ref)
    a2 = product(w2_ref) if swiglu else None
    if nk == 1:
        finish(a, a2)
    else:
        @pl.when(k == 0)
        def _():
            acc_ref[...] = a

        @pl.when(k > 0)
        def _():
            acc_ref[...] += a

        @pl.when(k == nk - 1)
        def _():
            finish(acc_ref[...], None)


def _matmul(x, w, *, tm, tn, tk=None, w2=None, pre=None, g=None, mods=None, mod_base=None, tok=None,
            shift_idx=None, scale_idx=None, epi=None, res=None, gate_idx=None, out_dtype=F32,
            x_col0=0, n_rows=None, row0=0, w_col0=0, n_cols=None, tiles=None, slab_in=False, slab_out=False,
            precise=False, name="mm"):
    grouped = tiles is not None
    kdim = w.shape[-2]
    n = (w.shape[-1] - w_col0) if n_cols is None else n_cols
    tk = kdim if tk is None else tk
    nk = kdim // tk
    assert kdim % tk == 0 and n % tn == 0 and w_col0 % tn == 0
    wcb0 = w_col0 // tn
    swiglu = w2 is not None
    assert nk == 1 or not (pre or swiglu)
    if slab_in:
        assert x.shape[1] == LANES and kdim % LANES == 0 and nk == 1 and not pre
        rows = x.shape[0] // (kdim // LANES) if n_rows is None else n_rows
    else:
        rows = (x.shape[0] - row0) if n_rows is None else n_rows
    assert rows % tm == 0 and row0 % tm == 0
    ni, nj = (tiles[0].shape[0] if grouped else rows // tm), n // tn
    rb0 = row0 // tm
    xcb0 = x_col0 // tk
    assert x_col0 % tk == 0

    def rowblk(i, pref):
        return pref[1][i] if grouped else i + rb0

    def grp(i):
        return tok.group_of(i * tm)

    in_specs, args = [], []
    if slab_in:
        in_specs.append(pl.BlockSpec((tm * (kdim // LANES), LANES), lambda i, j, k, *p: (rowblk(i, p), 0)))
    else:
        in_specs.append(pl.BlockSpec((tm, tk), lambda i, j, k, *p: (rowblk(i, p), xcb0 + k)))
    args.append(x)
    if pre:
        in_specs.append(pl.BlockSpec((1, tk), lambda i, j, k, *p: (0, 0)))
        args.append(g.reshape(1, tk).astype(F32))
    if pre == "normmod":
        for idx in (shift_idx, scale_idx):
            in_specs.append(pl.BlockSpec((1, 1, tk), lambda i, j, k, *p, idx=idx: (mod_base + grp(i) * 6 + idx, 0, 0)))
            args.append(mods)
    wspec = (pl.BlockSpec((None, tk, tn), lambda i, j, k, *p: (p[0][i], k, wcb0 + j)) if grouped
             else pl.BlockSpec((tk, tn), lambda i, j, k, *p: (k, wcb0 + j)))
    in_specs.append(wspec)
    args.append(w)
    if swiglu:
        in_specs.append(wspec)
        args.append(w2)
    if epi == "resgate":
        in_specs.append(pl.BlockSpec((tm, tn), lambda i, j, k, *p: (i + rb0, j)))
        args.append(res)
        in_specs.append(pl.BlockSpec((1, 1, tn), lambda i, j, k, *p: (mod_base + grp(i) * 6 + gate_idx, 0, j)))
        args.append(mods)
    if slab_out:
        assert tn == n and n % LANES == 0
        out_shape = jax.ShapeDtypeStruct((rows * (n // LANES), LANES), out_dtype)
        out_spec = pl.BlockSpec((tm * (n // LANES), LANES), lambda i, j, k, *p: (rowblk(i, p), 0))
    else:
        out_shape = jax.ShapeDtypeStruct((rows, n), out_dtype)
        out_spec = pl.BlockSpec((tm, tn), lambda i, j, k, *p: (rowblk(i, p) - rb0, j))
    scratch = []
    if pre or slab_in:
        scratch.append(pltpu.VMEM((tm, tk), BF16))
    if nk > 1:
        scratch.append(pltpu.VMEM((tm, tn), F32))
    body = functools.partial(_mm_body, pre=pre, swiglu=swiglu, epi=epi, nk=nk, nj=nj, grouped=grouped,
                             slab_in=slab_in, slab_out=slab_out, precise=precise, tn=tn)
    grid_spec = pltpu.PrefetchScalarGridSpec(
        num_scalar_prefetch=2 if grouped else 0, grid=(ni, nj, nk),
        in_specs=in_specs, out_specs=out_spec, scratch_shapes=scratch)
    call = pl.pallas_call(body, grid_spec=grid_spec, out_shape=out_shape,
                          compiler_params=_cparams(3), name=name)
    return call(*tiles, *args) if grouped else call(*args)


def _mod_body(c_ref, w_ref, b_ref, o_ref):
    cb = _silu(c_ref[...]).astype(BF16)
    o_ref[...] = _dot(cb, w_ref[...].astype(BF16)) + b_ref[...]


def _modulation(cond, w_mod, b_mod):
    depth, d, n = w_mod.shape
    gp = cond.shape[0]
    tn = _pick(n, (1024, 512, 256, 128))
    return pl.pallas_call(
        _mod_body,
        grid=(depth, n // tn),
        in_specs=[pl.BlockSpec((gp, d), lambda i, j: (0, 0)),
                  pl.BlockSpec((None, d, tn), lambda i, j: (i, 0, j)),
                  pl.BlockSpec((None, 1, tn), lambda i, j: (i, 0, j))],
        out_specs=pl.BlockSpec((None, gp, tn), lambda i, j: (i, 0, j)),
        out_shape=jax.ShapeDtypeStruct((depth, gp, n), F32),
        compiler_params=_cparams(2), name="modulation",
    )(cond, w_mod, b_mod.reshape(depth, 1, n))


def _rms_body(x_ref, g_ref, o_ref):
    xf = x_ref[...]
    ms = jnp.mean(xf * xf, axis=-1, keepdims=True)
    o_ref[...] = (xf * lax.rsqrt(ms + EPS)) * g_ref[...]


def _rmsnorm(x, g, tm):
    t, d = x.shape
    return pl.pallas_call(
        _rms_body, grid=(t // tm,),
        in_specs=[pl.BlockSpec((tm, d), lambda i: (i, 0)), pl.BlockSpec((1, d), lambda i: (0, 0))],
        out_specs=pl.BlockSpec((tm, d), lambda i: (i, 0)),
        out_shape=jax.ShapeDtypeStruct((t, d), F32),
        compiler_params=_cparams(1), name="final_norm",
    )(x, g.reshape(1, d))


def _dft_tables(l):
    f = jnp.arange(l, dtype=jnp.int32)
    m = (f[:, None] * f[None, :]) % (2 * l)
    ang = m.astype(F32) * (math.pi / l)
    return jnp.cos(ang), jnp.sin(ang)


def _hy_features(l):
    t = jnp.arange(l, dtype=F32)
    t01 = t / (l - 1)
    bands = jnp.linspace(1e-4, HY_BANDS - 1, HY_BANDS, dtype=F32)
    ang = (2.0 * math.pi / l) * t[:, None] * bands[None, :]
    z = jnp.concatenate([t01[:, None], jnp.cos(ang), jnp.sin(ang)], axis=-1)
    return jnp.pad(z, ((0, 0), (0, 128 - z.shape[1])))


def _hy_taps_body(z_ref, w1_ref, b1_ref, w2_ref, b2_ref, w3f_ref, w3b_ref, dec_ref, ksum_ref, kdif_ref, knyq_ref):
    l = z_ref.shape[0]
    dot = functools.partial(jnp.dot, preferred_element_type=F32, precision=HIGHEST)
    f = jnp.sin(HY_SIN_FREQ * (dot(z_ref[...], w1_ref[...]) + b1_ref[...]))
    f = jnp.sin(HY_SIN_FREQ * (dot(f, w2_ref[...]) + b2_ref[...]))
    row = lax.broadcasted_iota(jnp.int32, (l, 1), 0)
    t01 = row.astype(F32) / (l - 1)
    kf = dot(f, w3f_ref[...]) * jnp.exp(-t01 * dec_ref[0:1, :])
    kb = dot(f, w3b_ref[...]) * jnp.exp(-t01 * dec_ref[1:2, :])
    kb = jnp.where(row == 0, 0.0, kb)
    l1 = jnp.sum(jnp.abs(kf), axis=0, keepdims=True) + jnp.sum(jnp.abs(kb), axis=0, keepdims=True)
    kf = kf / l1
    kb = kb / l1
    ks = kf + kb
    alt = jnp.where(row % 2 == 0, 1.0, -1.0)
    ksum_ref[...] = ks
    kdif_ref[...] = kb - kf
    knyq_ref[...] = jnp.sum(ks * alt, axis=0, keepdims=True)


def _hy_taps(l, w1, b1, w2, b2, w3, decay):
    d = decay.shape[1]
    nf = w2.shape[0]
    cb = _pick(d, (512, 256, 128))
    z = _hy_features(l)
    w1p = jnp.pad(w1, ((0, 128 - w1.shape[0]), (0, 0)))
    full = lambda shape: pl.BlockSpec(shape, lambda c: (0,) * len(shape))
    return pl.pallas_call(
        _hy_taps_body, grid=(d // cb,),
        in_specs=[full((l, 128)), full((128, nf)), full((1, nf)), full((nf, nf)), full((1, nf)),
                  pl.BlockSpec((nf, cb), lambda c: (0, c)),
                  pl.BlockSpec((nf, cb), lambda c: (0, d // cb + c)),
                  pl.BlockSpec((2, cb), lambda c: (0, c))],
        out_specs=[pl.BlockSpec((l, cb), lambda c: (0, c)), pl.BlockSpec((l, cb), lambda c: (0, c)),
                   pl.BlockSpec((1, cb), lambda c: (0, c))],
        out_shape=[jax.ShapeDtypeStruct((l, d), F32), jax.ShapeDtypeStruct((l, d), F32),
                   jax.ShapeDtypeStruct((1, d), F32)],
        compiler_params=_cparams(1), name="hyena_taps",
    )(z, w1p, b1.reshape(1, nf), w2, b2.reshape(1, nf), w3, w3, decay)


def _dwconv3(a, w_ref, b_ref, row, l):
    prev = jnp.where(row == 0, 0.0, pltpu.roll(a, 1, axis=0))
    nxt = jnp.where(row == l - 1, 0.0, pltpu.roll(a, l - 1, axis=0))
    return prev * w_ref[0:1, :] + a * w_ref[1:2, :] + nxt * w_ref[2:3, :] + b_ref[...]


def _hy_core_body(x0_ref, x1_ref, v_ref, w0_ref, w1_ref, wv_ref, b0_ref, b1_ref, bv_ref, skip_ref,
                  kr_ref, ki_ref, knyq_ref, c_ref, s_ref, o_ref):
    l = x0_ref.shape[0]
    row = lax.broadcasted_iota(jnp.int32, (l, 1), 0)
    x0 = _dwconv3(x0_ref[...], w0_ref, b0_ref, row, l)
    x1 = _dwconv3(x1_ref[...], w1_ref, b1_ref, row, l)
    v = _dwconv3(v_ref[...], wv_ref, bv_ref, row, l)
    z = v * x1
    zb = z.astype(BF16)
    cm = c_ref[...]
    sm = s_ref[...]
    zr = _dot(cm, zb)
    zs = _dot(sm, zb)
    wf = jnp.where(row == 0, 0.5 / l, 1.0 / l)
    kr = kr_ref[...] * wf
    ki = ki_ref[...] * wf
    a = (zr * kr + zs * ki).astype(BF16)
    b = (zs * kr - zr * ki).astype(BF16)
    alt = jnp.where(row % 2 == 0, 1.0, -1.0)
    znyq = jnp.sum(z * alt, axis=0, keepdims=True)
    y = _dot(cm, a) + _dot(sm, b) + alt * (znyq * knyq_ref[...] * (0.5 / l))
    o_ref[...] = (x0 * (y + z * skip_ref[...])).astype(o_ref.dtype)


def _hy_core(u, row0, nseq, l, d, conv_w, conv_b, skip, kr, ki, knyq, cmat, smat):
    cb = _pick(d, (128,) if l > 512 else (512, 256, 128))
    nd = d // cb
    sb0 = row0 // l
    assert row0 % l == 0
    useq = lambda sec: pl.BlockSpec((l, cb), lambda s, c, sec=sec: (sb0 + s, sec * nd + c))
    wsec = lambda sec: pl.BlockSpec((3, cb), lambda s, c, sec=sec: (0, sec * nd + c))
    bsec = lambda sec: pl.BlockSpec((1, cb), lambda s, c, sec=sec: (0, sec * nd + c))
    col = lambda rows: pl.BlockSpec((rows, cb), lambda s, c: (0, c))
    const = pl.BlockSpec((l, l), lambda s, c: (0, 0), pipeline_mode=pl.Buffered(1))
    cb3 = conv_b.reshape(1, 3 * d)
    return pl.pallas_call(
        _hy_core_body, grid=(nseq, nd),
        in_specs=[useq(0), useq(1), useq(2), wsec(0), wsec(1), wsec(2), bsec(0), bsec(1), bsec(2),
                  col(1), col(l), col(l), col(1), const, const],
        out_specs=pl.BlockSpec((l, cb), lambda s, c: (s, c)),
        out_shape=jax.ShapeDtypeStruct((nseq * l, d), BF16),
        compiler_params=_cparams(2), name=f"hyena_core_{l}",
    )(u, u, u, conv_w, conv_w, conv_w, cb3, cb3, cb3, skip.reshape(1, d), kr, ki, knyq, cmat, smat)


def _hyena(tok, x, mods, mod_base, g, w_in, conv_w, conv_b, f_w1, f_b1, f_w2, f_b2, f_w3, decay, skip, w_out):
    d = x.shape[1]
    tm = tok.row_tile()
    u = _matmul(x, w_in, tm=tm, tn=_pick(3 * d, (512, 256, 128)), pre="normmod", g=g, mods=mods,
                mod_base=mod_base, tok=tok, shift_idx=0, scale_idx=1, name="hyena_in")
    parts = []
    for row0, nseq, l in ((0, tok.bp, tok.lp), (tok.tp, tok.bs, tok.ls)):
        ksum, kdif, knyq = _hy_taps(l, f_w1, f_b1, f_w2, f_b2, f_w3, decay)
        cmat, smat = _dft_tables(l)
        tf = _pick(l, (512, 256, 128))
        tn = _pick(d, (512, 256, 128))
        kr = _matmul(cmat, ksum, tm=tf, tn=tn, precise=True, name="hyena_spec_re")
        ki = _matmul(smat, kdif, tm=tf, tn=tn, precise=True, name="hyena_spec_im")
        parts.append(_hy_core(u, row0, nseq, l, d, conv_w, conv_b, skip, kr, ki, knyq,
                              cmat.astype(BF16), smat.astype(BF16)))
    gmix = jnp.concatenate(parts, axis=0)
    return _matmul(gmix, w_out, tm=tm, tn=_pick(d, (512, 256, 128)), epi="resgate", res=x, mods=mods,
                   mod_base=mod_base, tok=tok, gate_idx=2, name="hyena_out")


def _ssd_conv_body(x_ref, w_ref, b_ref, o_ref):
    l = x_ref.shape[0]
    row = lax.broadcasted_iota(jnp.int32, (l, 1), 0)
    o_ref[...] = _silu(_dwconv3(x_ref[...], w_ref, b_ref, row, l))


def _ssd_conv(proj, col0, conv_w, conv_b, row0, nseq, l):
    cdim = conv_w.shape[1]
    cb = _pick(math.gcd(cdim, col0), (512, 256, 128))
    cb0, sb0 = col0 // cb, row0 // l
    return pl.pallas_call(
        _ssd_conv_body, grid=(nseq, cdim // cb),
        in_specs=[pl.BlockSpec((l, cb), lambda s, c: (sb0 + s, cb0 + c)),
                  pl.BlockSpec((3, cb), lambda s, c: (0, c)),
                  pl.BlockSpec((1, cb), lambda s, c: (0, c))],
        out_specs=pl.BlockSpec((l, cb), lambda s, c: (s, c)),
        out_shape=jax.ShapeDtypeStruct((nseq * l, cdim), F32),
        compiler_params=_cparams(2), name=f"ssd_conv_{l}",
    )(proj, conv_w, conv_b.reshape(1, cdim))


def _ssd_dt_body(raw_ref, bias_ref, alog_ref, dt_ref, cum_ref):
    tm, w = raw_ref.shape
    xr = raw_ref[...] + bias_ref[...]
    dt = jnp.maximum(xr, 0.0) + jnp.log1p(jnp.exp(-jnp.abs(xr)))
    dt_ref[...] = dt
    a = dt * (-jnp.exp(alog_ref[...]))
    ri = lax.broadcasted_iota(jnp.int32, (SSD_CHUNK, SSD_CHUNK), 0)
    ci = lax.broadcasted_iota(jnp.int32, (SSD_CHUNK, SSD_CHUNK), 1)
    lower = (ci <= ri).astype(F32)
    upper = (ci >= ri).astype(F32)
    lane = lax.broadcasted_iota(jnp.int32, (SSD_CHUNK, w), 1)
    for q in range(tm // SSD_CHUNK):
        ac = a[q * SSD_CHUNK:(q + 1) * SSD_CHUNK, :]
        pc = jnp.dot(lower, ac, preferred_element_type=F32, precision=HIGHEST)
        rc = jnp.dot(upper, ac, preferred_element_type=F32, precision=HIGHEST)
        cum_ref[q * SSD_CHUNK:(q + 1) * SSD_CHUNK, :] = jnp.where(lane < w // 2, pc, rc)


def _ssd_dt(proj, col0, dt_bias, a_log, tm):
    t = proj.shape[0]
    w = dt_bias.size
    assert col0 % w == 0
    return pl.pallas_call(
        _ssd_dt_body, grid=(t // tm,),
        in_specs=[pl.BlockSpec((tm, w), lambda i: (i, col0 // w)),
                  pl.BlockSpec((1, w), lambda i: (0, 0)), pl.BlockSpec((1, w), lambda i: (0, 0))],
        out_specs=[pl.BlockSpec((tm, w), lambda i: (i, 0)), pl.BlockSpec((tm, w), lambda i: (i, 0))],
        out_shape=[jax.ShapeDtypeStruct((t, w), F32), jax.ShapeDtypeStruct((t, w), F32)],
        compiler_params=_cparams(1), name="ssd_dt",
    )(proj, dt_bias.reshape(1, w), a_log.reshape(1, w))


def _ssd_scan_body(*refs, has_init, r, p, nc):
    refs = list(refs)
    x_ref, b_ref, c_ref, dsk_ref = refs[:4]
    dt_refs, cum_refs, cumt_refs = refs[4:6], refs[6:8], refs[8:10]
    refs = refs[10:]
    init_refs = (refs.pop(0), refs.pop(0)) if has_init else None
    y_ref, fin_refs, st_ref = refs[0], refs[1:3], refs[3]
    ch = SSD_CHUNK
    npair = r // 2
    lane = lax.broadcasted_iota(jnp.int32, (ch, 2 * p), 1)
    left = lane < p
    ri = lax.broadcasted_iota(jnp.int32, (ch, ch), 0)
    ci = lax.broadcasted_iota(jnp.int32, (ch, ch), 1)

    for d in range(2):
        keep = (ci <= ri) if d == 0 else (ci >= ri)
        for pr in range(npair):
            if has_init:
                st_ref[pr] = init_refs[d][pr * 2 * p:(pr + 1) * 2 * p, :].T
            else:
                st_ref[pr] = jnp.zeros((st_ref.shape[1], 2 * p), F32)

        def chunk(ic, carry, d=d, keep=keep):
            cidx = ic if d == 0 else nc - 1 - ic
            r0 = pl.multiple_of(cidx * ch, ch)
            rows = pl.ds(r0, ch)
            bc = b_ref[rows, :].astype(BF16)
            cc = c_ref[rows, :].astype(BF16)
            cbm = _dot_nt(cc, bc)
            bt = b_ref[rows, :].T.astype(BF16)
            dtc = dt_refs[d][rows, :]
            cumc = cum_refs[d][rows, :]
            cumt = cumt_refs[d][:, rows]
            edge = cumc[0:1, :] if d == 1 else cumc[ch - 1:ch, :]
            for pr in range(npair):
                xpair = x_ref[rows, pr * 2 * p:(pr + 1) * 2 * p]
                ha, hb = 2 * pr, 2 * pr + 1
                sel = lambda col: jnp.where(left, col[:, ha:ha + 1], col[:, hb:hb + 1])
                xs = xpair * sel(dtc)
                xsb = xs.astype(BF16)
                ydiag = []
                for h in (ha, hb):
                    seg = cumc[:, h:h + 1] - cumt[h:h + 1, :]
                    dec = jnp.exp(jnp.where(keep, seg, -jnp.inf))
                    ydiag.append(_dot((cbm * dec).astype(BF16), xsb))
                cum_pair = sel(cumc)
                st = st_ref[pr]
                yoff = _dot(cc, st.astype(BF16)) * jnp.exp(cum_pair)
                yc = jnp.where(left, ydiag[0], ydiag[1]) + yoff
                edge_pair = jnp.where(left[0:1, :], edge[:, ha:ha + 1], edge[:, hb:hb + 1])
                xsd = (xs * jnp.exp(edge_pair - cum_pair)).astype(BF16)
                st_ref[pr] = st * jnp.exp(edge_pair) + _dot(bt, xsd)
                cols = slice(pr * 2 * p, (pr + 1) * 2 * p)
                if d == 0:
                    y_ref[rows, cols] = yc + xpair * dsk_ref[:, cols]
                else:
                    y_ref[rows, cols] += yc
            return carry

        lax.fori_loop(0, nc, chunk, 0)
        for pr in range(npair):
            fin_refs[d][pr * 2 * p:(pr + 1) * 2 * p, :] = st_ref[pr].T


def _ssd_scan(xbc, dtg, cumg, cumtg, d_vec, nseq, l, di, g, n, p, init):
    h = di // p
    r = h // g
    rp = r * p
    assert r % 2 == 0 and rp % 128 == 0 and n % 128 == 0 and l % SSD_CHUNK == 0
    nb0, nc0 = di // n, (di + g * n) // n
    has_init = init is not None
    dspec = lambda d: pl.BlockSpec((None, l, r), lambda s, gi, d=d: (d * g + gi, s, 0))
    tspec = lambda d: pl.BlockSpec((None, r, l), lambda s, gi, d=d: (d * g + gi, 0, s))
    sspec = pl.BlockSpec((None, rp, n), lambda s, gi: (s, gi, 0))
    in_specs = [pl.BlockSpec((l, rp), lambda s, gi: (s, gi)),
                pl.BlockSpec((l, n), lambda s, gi: (s, nb0 + gi)),
                pl.BlockSpec((l, n), lambda s, gi: (s, nc0 + gi)),
                pl.BlockSpec((1, rp), lambda s, gi: (0, gi)),
                dspec(0), dspec(1), dspec(0), dspec(1), tspec(0), tspec(1)]
    args = [xbc, xbc, xbc, d_vec, dtg, dtg, cumg, cumg, cumtg, cumtg]
    if has_init:
        in_specs += [sspec, sspec]
        args += [s.reshape(nseq, h * p, n) for s in init]
    st_shape = jax.ShapeDtypeStruct((nseq, h * p, n), F32)
    body = functools.partial(_ssd_scan_body, has_init=has_init, r=r, p=p, nc=l // SSD_CHUNK)
    y, sf, sb = pl.pallas_call(
        body, grid=(nseq, g), in_specs=in_specs,
        out_specs=[pl.BlockSpec((l, rp), lambda s, gi: (s, gi)), sspec, sspec],
        out_shape=[jax.ShapeDtypeStruct((nseq * l, di), F32), st_shape, st_shape],
        scratch_shapes=[pltpu.VMEM((r // 2, n, 2 * p), F32)],
        compiler_params=_cparams(2), name=f"ssd_scan_{l}",
    )(*args)
    return y, sf.reshape(nseq, h, p, n), sb.reshape(nseq, h, p, n)


def _ssd_gate_body(y_ref, z_ref, g_ref, o_ref):
    v = y_ref[...] * _silu(z_ref[...])
    ms = jnp.mean(v * v, axis=-1, keepdims=True)
    o_ref[...] = ((v * lax.rsqrt(ms + EPS)) * g_ref[...]).astype(o_ref.dtype)


def _ssd_gate(y, proj, norm_g, tm):
    t, di = y.shape
    return pl.pallas_call(
        _ssd_gate_body, grid=(t // tm,),
        in_specs=[pl.BlockSpec((tm, di), lambda i: (i, 0)), pl.BlockSpec((tm, di), lambda i: (i, 0)),
                  pl.BlockSpec((1, di), lambda i: (0, 0))],
        out_specs=pl.BlockSpec((tm, di), lambda i: (i, 0)),
        out_shape=jax.ShapeDtypeStruct((t, di), BF16),
        compiler_params=_cparams(1), name="ssd_gate",
    )(y, proj, norm_g.reshape(1, di))


def _ssd(tok, x, mods, mod_base, gnorm, w_in, conv_w, conv_b, dt_bias, a_log, d_skip, norm_g, w_out,
         init_f, init_b):
    d = x.shape[1]
    p, n = init_f.shape[-2], init_f.shape[-1]
    h = dt_bias.shape[1]
    di = h * p
    cdim = conv_w.shape[1]
    g = (cdim - di) // (2 * n)
    r = h // g
    tm = tok.row_tile()
    pre_kw = dict(pre="normmod", g=gnorm, mods=mods, mod_base=mod_base, tok=tok, shift_idx=0, scale_idx=1)
    proj = _matmul(x, w_in, tm=tm, tn=_pick(math.gcd(di, cdim), (512, 256, 128)), n_cols=di + cdim,
                   name="ssd_in", **pre_kw)
    dt_raw = _matmul(x, w_in, tm=tm, tn=2 * h, w_col0=di + cdim, name="ssd_in_dt", **pre_kw)
    dt, cum = _ssd_dt(dt_raw, 0, dt_bias, a_log, _pick(tok.t, (512, 256, 128)))
    split = lambda a: jnp.transpose(a.reshape(tok.t, 2 * g, r), (1, 0, 2))
    dtg, cumg = split(dt), split(cum)
    cumtg = jnp.transpose(cumg, (0, 2, 1))
    d_vec = jnp.repeat(d_skip.astype(F32), p).reshape(1, di)
    ys, states = [], []
    for row0, nseq, l, init in ((0, tok.bp, tok.lp, None), (tok.tp, tok.bs, tok.ls, (init_f, init_b))):
        xbc = _ssd_conv(proj, di, conv_w, conv_b, row0, nseq, l)
        sl = slice(row0, row0 + nseq * l)
        y, sf, sb = _ssd_scan(xbc, dtg[:, sl], cumg[:, sl], cumtg[:, :, sl], d_vec, nseq, l, di, g, n, p, init)
        ys.append(y)
        states.append((sf, sb))
    yg = _ssd_gate(jnp.concatenate(ys, axis=0), proj, norm_g, _pick(tok.t, (256, 128)))
    out = _matmul(yg, w_out, tm=tm, tn=_pick(d, (512, 256, 128)), tk=_pick(di, (2048, 1024, 512, 256)),
                  epi="resgate", res=x, mods=mods, mod_base=mod_base, tok=tok, gate_idx=2, name="ssd_out")
    return out, states[0]


def _rope_tables(n_tokens, dr):
    axis_dim = dr // 2
    rows = n_tokens // GRID_W
    row = jnp.broadcast_to(jnp.arange(rows, dtype=F32)[:, None], (rows, GRID_W)).reshape(-1)
    col = jnp.broadcast_to(jnp.arange(GRID_W, dtype=F32)[None, :], (rows, GRID_W)).reshape(-1)
    inv_freq = ROPE_BASE ** (-jnp.arange(0, axis_dim, 2, dtype=F32) / axis_dim)
    ang = jnp.concatenate([row[:, None] * inv_freq, col[:, None] * inv_freq], axis=-1)
    return jnp.cos(ang), jnp.sin(ang)


def _rot_cols(w, dr):
    half = dr // 2
    return jnp.concatenate([-w[..., half:], w[..., :half]], axis=-1)


def _mla_kv_body(kv_ref, pe_ref, g_ref, cs_ref, ckv_ref, kpe_ref):
    kv = kv_ref[...]
    ms = jnp.mean(kv * kv, axis=-1, keepdims=True)
    ckv_ref[...] = (kv * lax.rsqrt(ms + EPS)) * g_ref[...]
    prod = pe_ref[...] * cs_ref[...]
    kpe_ref[...] = prod + pltpu.roll(prod, prod.shape[1] // 2, axis=1)


def _mla_kv(qkv, col0, rank, dr, g, cs, tm):
    t = qkv.shape[0]
    assert col0 % rank == 0 and (col0 + rank) % (2 * dr) == 0 and 2 * dr == 128
    return pl.pallas_call(
        _mla_kv_body, grid=(t // tm,),
        in_specs=[pl.BlockSpec((tm, rank), lambda i: (i, col0 // rank)),
                  pl.BlockSpec((tm, 2 * dr), lambda i: (i, (col0 + rank) // (2 * dr))),
                  pl.BlockSpec((1, rank), lambda i: (0, 0)),
                  pl.BlockSpec((tm, 2 * dr), lambda i: (i, 0))],
        out_specs=[pl.BlockSpec((tm, rank), lambda i: (i, 0)), pl.BlockSpec((tm, 2 * dr), lambda i: (i, 0))],
        out_shape=[jax.ShapeDtypeStruct((t, rank), F32), jax.ShapeDtypeStruct((t, 2 * dr), F32)],
        compiler_params=_cparams(1), name="mla_kv",
    )(qkv, qkv, g.reshape(1, rank), cs)


def _attn_body(qn_ref, qp_ref, qr_ref, cos_ref, sin_ref, kv_ref, kpe_ref, o_ref, *, dn, dr, dv, scale):
    qrot = qp_ref[...] * cos_ref[...] + qr_ref[...] * sin_ref[...]
    lane = lax.broadcasted_iota(jnp.int32, qrot.shape, 1)
    kpe = kpe_ref[...].astype(BF16)
    for h in range(2):
        qn = qn_ref[:, h * dn:(h + 1) * dn].astype(BF16)
        kn = kv_ref[:, h * (dn + dv):h * (dn + dv) + dn]
        v = kv_ref[:, h * (dn + dv) + dn:(h + 1) * (dn + dv)]
        qh = jnp.where((lane >= h * dr) & (lane < (h + 1) * dr), qrot, 0.0).astype(BF16)
        s = (_dot_nt(qn, kn) + _dot_nt(qh, kpe)) * scale
        m = jnp.max(s, axis=-1, keepdims=True)
        e = jnp.exp(s - m)
        pn = e / jnp.sum(e, axis=-1, keepdims=True)
        o_ref[:, h * dv:(h + 1) * dv] = _dot(pn.astype(BF16), v).astype(o_ref.dtype)


def _attention(q, cosq, sinq, kv, kpe, row0, nb, lq, lk, heads, dn, dr, dv):
    assert dn == 128 and dv == 128 and dr == 64 and heads % 2 == 0
    tq = _pick(lq, (512, 256, 128))
    nq = lq // tq
    rb0 = row0 // tq
    hp = heads // 2
    pe0, rot0 = heads * dn // 128, (heads * dn + heads * dr) // 128
    body = functools.partial(_attn_body, dn=dn, dr=dr, dv=dv, scale=(dn + dr) ** -0.5)
    return pl.pallas_call(
        body, grid=(nb, hp, nq),
        in_specs=[pl.BlockSpec((tq, 2 * dn), lambda b, h, i: (rb0 + b * nq + i, h)),
                  pl.BlockSpec((tq, 128), lambda b, h, i: (rb0 + b * nq + i, pe0 + h)),
                  pl.BlockSpec((tq, 128), lambda b, h, i: (rb0 + b * nq + i, rot0 + h)),
                  pl.BlockSpec((tq, 128), lambda b, h, i: (rb0 + b * nq + i, 0)),
                  pl.BlockSpec((tq, 128), lambda b, h, i: (rb0 + b * nq + i, 0)),
                  pl.BlockSpec((None, lk, 2 * (dn + dv)), lambda b, h, i: (b, 0, h)),
                  pl.BlockSpec((None, lk, 128), lambda b, h, i: (b, 0, 0))],
        out_specs=pl.BlockSpec((tq, 2 * dv), lambda b, h, i: (b * nq + i, h)),
        out_shape=jax.ShapeDtypeStruct((nb * lq, heads * dv), BF16),
        compiler_params=_cparams(3), name=f"mla_attn_{lq}",
    )(q, q, q, cosq, sinq, kv, kpe)


def _mla(tok, x, mods, mod_base, gnorm, w_dq, q_norm_g, w_uq, w_dkv, kv_norm_g, w_ukv, w_o, cache_ckv, cache_kpe):
    d = x.shape[1]
    heads = MLA_HEADS
    qr_, kvr, dr = w_dq.shape[1], kv_norm_g.shape[0], cache_kpe.shape[-1]
    dv = w_o.shape[0] // heads
    dn = w_ukv.shape[1] // heads - dv
    past = cache_ckv.shape[1]
    tm = tok.row_tile()
    w_a = jnp.concatenate([w_dq, w_dkv, _rot_cols(w_dkv[:, kvr:], dr)], axis=1)
    wq = w_uq.reshape(qr_, heads, dn + dr)
    wq_pe = wq[:, :, dn:]
    w_q = jnp.concatenate([wq[:, :, :dn].reshape(qr_, heads * dn), wq_pe.reshape(qr_, heads * dr),
                           _rot_cols(wq_pe, dr).reshape(qr_, heads * dr)], axis=1)
    qkv = _matmul(x, w_a, tm=tm, tn=_pick(w_a.shape[1], (384, 256, 128)), pre="normmod", g=gnorm, mods=mods,
                  mod_base=mod_base, tok=tok, shift_idx=0, scale_idx=1, name="mla_down")
    cos, sin = _rope_tables(tok.ls, dr)
    cos2 = jnp.tile(jnp.concatenate([cos, cos], axis=1), (tok.bs, 1))
    sin2 = jnp.tile(jnp.concatenate([sin, sin], axis=1), (tok.bs, 1))
    ones, zeros = jnp.ones((tok.tp, dr), F32), jnp.zeros((tok.tp, dr), F32)
    cos_t = jnp.concatenate([ones, cos2], axis=0)
    sin_t = jnp.concatenate([zeros, sin2], axis=0)
    ckv, kpe = _mla_kv(qkv, qr_, kvr, dr, kv_norm_g, jnp.concatenate([cos_t, sin_t], axis=1), tm)
    q = _matmul(qkv, w_q, tm=tm, tn=_pick(w_q.shape[1], (512, 256, 128)), tk=qr_, pre="norm", g=q_norm_g,
                name="mla_q")
    cosq = jnp.concatenate([cos_t, cos_t], axis=1)
    sinq = jnp.concatenate([sin_t, sin_t], axis=1)
    ckv_s = jnp.concatenate([cache_ckv, ckv[tok.tp:].reshape(tok.bs, tok.ls, kvr)], axis=1)
    kpe_s = jnp.concatenate([jnp.tile(cache_kpe, (1, 1, 2)), kpe[tok.tp:].reshape(tok.bs, tok.ls, 2 * dr)], axis=1)
    lk_s = past + tok.ls
    ckv_all = jnp.concatenate([ckv[:tok.tp], ckv_s.reshape(tok.bs * lk_s, kvr)], axis=0)
    kv = _matmul(ckv_all, w_ukv, tm=_pick(math.gcd(tok.tp, tok.bs * lk_s), (1024, 512, 256, 128)),
                 tn=_pick(w_ukv.shape[1], (512, 256, 128)), out_dtype=BF16, name="mla_up_kv")
    nkv = kv.shape[1]
    o_p = _attention(q, cosq, sinq, kv[:tok.tp].reshape(tok.bp, tok.lp, nkv),
                     kpe[:tok.tp].reshape(tok.bp, tok.lp, 2 * dr), 0, tok.bp, tok.lp, tok.lp, heads, dn, dr, dv)
    o_s = _attention(q, cosq, sinq, kv[tok.tp:].reshape(tok.bs, lk_s, nkv), kpe_s, tok.tp, tok.bs, tok.ls, lk_s,
                     heads, dn, dr, dv)
    o = jnp.concatenate([o_p, o_s], axis=0)
    out = _matmul(o, w_o, tm=tm, tn=_pick(d, (512, 256, 128)), epi="resgate", res=x, mods=mods,
                  mod_base=mod_base, tok=tok, gate_idx=2, name="mla_out")
    new_ckv = ckv[:tok.tp].reshape(tok.bp, tok.lp, kvr)
    new_kpe = qkv[:tok.tp, qr_ + kvr:qr_ + kvr + dr].reshape(tok.bp, tok.lp, dr)
    return out, new_ckv, new_kpe


def _dense_ffn(tok, x, mods, mod_base, gnorm, w_gate, w_up, w_down):
    d, dff = w_gate.shape
    tm = tok.row_tile()
    hmid = _matmul(x, w_gate, w2=w_up, tm=tm, tn=_pick(dff, (256, 128)), pre="normmod", g=gnorm, mods=mods,
                   mod_base=mod_base, tok=tok, shift_idx=3, scale_idx=4, out_dtype=BF16, name="ffn_up")
    return _matmul(hmid, w_down, tm=tm, tn=_pick(d, (512, 256, 128)), tk=_pick(dff, (1024, 512, 256, 128)),
                   epi="resgate", res=x, mods=mods, mod_base=mod_base, tok=tok, gate_idx=5, name="ffn_down")


def _route_body(x_ref, g_ref, sh_ref, sc_ref, wr_ref, br_ref, h_ref, info_ref, cnt_ref, carry_ref, *, n_exp):
    i = pl.program_id(0)
    tm, d = x_ref.shape
    slab = d // LANES

    @pl.when(i == 0)
    def _():
        carry_ref[...] = jnp.zeros_like(carry_ref)

    xf = x_ref[...]
    ms = jnp.mean(xf * xf, axis=-1, keepdims=True)
    hh = ((xf * lax.rsqrt(ms + EPS)) * g_ref[...]) * (1.0 + sc_ref[0]) + sh_ref[0]
    for s in range(slab):
        h_ref[pl.ds(s, tm, stride=slab), :] = hh[:, s * LANES:(s + 1) * LANES]
    logits = jnp.dot(hh, wr_ref[...], preferred_element_type=F32, precision=HIGHEST) + br_ref[...]
    lane = lax.broadcasted_iota(jnp.int32, logits.shape, 1)
    logits = jnp.where(lane < n_exp, logits, -jnp.inf)
    t1 = jnp.max(logits, axis=-1, keepdims=True)
    i1 = jnp.min(jnp.where(logits == t1, lane, 128), axis=-1, keepdims=True)
    rest = jnp.where(lane == i1, -jnp.inf, logits)
    t2 = jnp.max(rest, axis=-1, keepdims=True)
    i2 = jnp.min(jnp.where(rest == t2, lane, 128), axis=-1, keepdims=True)
    e2 = jnp.exp(t2 - t1)
    g1 = 1.0 / (1.0 + e2)
    g2 = e2 / (1.0 + e2)
    oh1 = lane == i1
    oh2 = lane == i2
    cnt = jnp.where(oh1 | oh2, 1.0, 0.0).astype(BF16)
    ri = lax.broadcasted_iota(jnp.int32, (tm, tm), 0)
    ci = lax.broadcasted_iota(jnp.int32, (tm, tm), 1)
    before = _dot((ci < ri).astype(BF16), cnt) + carry_ref[...]
    r1 = jnp.sum(jnp.where(oh1, before, 0.0), axis=-1, keepdims=True)
    r2 = jnp.sum(jnp.where(oh2, before, 0.0), axis=-1, keepdims=True)
    carry_ref[...] += jnp.sum(cnt.astype(F32), axis=0, keepdims=True)
    vals = (i1.astype(F32), i2.astype(F32), r1, r2, g1, g2)
    info = jnp.zeros(logits.shape, F32)
    for q, v in enumerate(vals):
        info = jnp.where(lane == q, v, info)
    info_ref[...] = info
    cnt_ref[...] = carry_ref[...]


def _moe_route(tok, x, mods, mod_base, gnorm, w_router, b_router, tm):
    t, d = x.shape
    n_exp = w_router.shape[1]
    wr = jnp.pad(w_router, ((0, 0), (0, 128 - n_exp)))
    br = jnp.pad(b_router, (0, 128 - n_exp)).reshape(1, 128)
    slab = d // LANES
    modspec = lambda idx: pl.BlockSpec((1, 1, d), lambda i: (mod_base + tok.group_of(i * tm) * 6 + idx, 0, 0))
    return pl.pallas_call(
        functools.partial(_route_body, n_exp=n_exp), grid=(t // tm,),
        in_specs=[pl.BlockSpec((tm, d), lambda i: (i, 0)), pl.BlockSpec((1, d), lambda i: (0, 0)),
                  modspec(3), modspec(4),
                  pl.BlockSpec((d, 128), lambda i: (0, 0)), pl.BlockSpec((1, 128), lambda i: (0, 0))],
        out_specs=[pl.BlockSpec((tm * slab, LANES), lambda i: (i, 0)),
                   pl.BlockSpec((tm, 128), lambda i: (i, 0)),
                   pl.BlockSpec((1, 128), lambda i: (0, 0))],
        out_shape=[jax.ShapeDtypeStruct((t * slab, LANES), F32), jax.ShapeDtypeStruct((t, 128), F32),
                   jax.ShapeDtypeStruct((1, 128), F32)],
        scratch_shapes=[pltpu.VMEM((1, 128), F32)],
        compiler_params=_cparams(1), name="moe_route",
    )(x, gnorm.reshape(1, d), mods, mods, wr, br)


def _dispatch_body(dest_ref, h_hbm, init_hbm, o_hbm, sem, *, tb, slab):
    del init_hbm
    base = pl.program_id(0) * tb

    def copy(a):
        t = (base + a) // 2
        return pltpu.make_async_copy(h_hbm.at[pl.ds(t * slab, slab)],
                                     o_hbm.at[pl.ds(dest_ref[base + a] * slab, slab)], sem)

    def start(a, c):
        copy(a).start()
        return c

    def wait(a, c):
        copy(a).wait()
        return c

    lax.fori_loop(0, tb, start, 0)
    lax.fori_loop(0, tb, wait, 0)


def _moe_dispatch(h_slab, dest, n_rows, slab):
    na = dest.shape[0]
    tb = _pick(na, (2048, 1024, 512, 256, 128, 64, 32, 16, 8, 4, 2))
    init = jnp.zeros((n_rows * slab, LANES), F32)
    grid_spec = pltpu.PrefetchScalarGridSpec(
        num_scalar_prefetch=1, grid=(na // tb,),
        in_specs=[pl.BlockSpec(memory_space=pl.ANY), pl.BlockSpec(memory_space=pl.ANY)],
        out_specs=pl.BlockSpec(memory_space=pl.ANY),
        scratch_shapes=[pltpu.SemaphoreType.DMA(())])
    return pl.pallas_call(
        functools.partial(_dispatch_body, tb=tb, slab=slab), grid_spec=grid_spec,
        out_shape=jax.ShapeDtypeStruct(init.shape, F32), input_output_aliases={2: 0},
        compiler_params=_cparams(1), name="moe_dispatch",
    )(dest, h_slab, init)


def _combine_body(pos_ref, x_ref, gate_ref, info_ref, y_hbm, o_ref, buf1, buf2, sem, *, tm):
    base = pl.program_id(0) * tm
    slab = buf1.shape[0] // tm

    def copies(r):
        rows = pl.ds(r * slab, slab)
        src = lambda k: y_hbm.at[pl.ds(pos_ref[2 * (base + r) + k] * slab, slab)]
        return (pltpu.make_async_copy(src(0), buf1.at[rows], sem),
                pltpu.make_async_copy(src(1), buf2.at[rows], sem))

    def start(r, c):
        for cp in copies(r):
            cp.start()
        return c

    def wait(r, c):
        for cp in copies(r):
            cp.wait()
        return c

    lax.fori_loop(0, tm, start, 0)
    lax.fori_loop(0, tm, wait, 0)
    g1 = info_ref[:, 4:5]
    g2 = info_ref[:, 5:6]
    for s in range(slab):
        cols = slice(s * LANES, (s + 1) * LANES)
        rows = pl.ds(s, tm, stride=slab)
        mix = buf1[rows, :] * g1 + buf2[rows, :] * g2
        o_ref[:, cols] = x_ref[:, cols] + gate_ref[0][:, cols] * mix


def _moe_combine(tok, x, mods, mod_base, info, y_slab, pos, tm):
    t, d = x.shape
    slab = d // LANES
    grid_spec = pltpu.PrefetchScalarGridSpec(
        num_scalar_prefetch=1, grid=(t // tm,),
        in_specs=[pl.BlockSpec((tm, d), lambda i, p: (i, 0)),
                  pl.BlockSpec((1, 1, d), lambda i, p: (mod_base + tok.group_of(i * tm) * 6 + 5, 0, 0)),
                  pl.BlockSpec((tm, 128), lambda i, p: (i, 0)),
                  pl.BlockSpec(memory_space=pl.ANY)],
        out_specs=pl.BlockSpec((tm, d), lambda i, p: (i, 0)),
        scratch_shapes=[pltpu.VMEM((tm * slab, LANES), F32), pltpu.VMEM((tm * slab, LANES), F32),
                        pltpu.SemaphoreType.DMA(())])
    return pl.pallas_call(
        functools.partial(_combine_body, tm=tm), grid_spec=grid_spec,
        out_shape=jax.ShapeDtypeStruct((t, d), F32),
        compiler_params=_cparams(1), name="moe_combine",
    )(pos, x, mods, info, y_slab)


def _moe_ffn(tok, x, mods, mod_base, gnorm, w_router, b_router, w_gate, w_up, w_down):
    t, d = x.shape
    n_exp, _, dexp = w_gate.shape
    tm = tok.row_tile()
    h_slab, info, counts = _moe_route(tok, x, mods, mod_base, gnorm, w_router, b_router, tm)
    te = _pick(2 * t, (1024, 512, 256, 128, 64, 32, 16, 8))
    counts = counts[0, :n_exp].astype(jnp.int32)
    padded = (counts + te - 1) // te * te
    pad_end = jnp.cumsum(padded)
    pad_start = pad_end - padded
    idx = info[:, 0:2].astype(jnp.int32)
    rank = info[:, 2:4].astype(jnp.int32)
    pos = (pad_start[idx] + rank).reshape(-1)
    n_tiles = 2 * t // te + n_exp
    n_rows = n_tiles * te
    tile_start = jnp.arange(n_tiles, dtype=jnp.int32) * te
    used = pad_end[-1] // te
    tile_row = jnp.minimum(jnp.arange(n_tiles, dtype=jnp.int32), used - 1)
    tile_exp = jnp.minimum(jnp.searchsorted(pad_end, tile_row * te, side="right"), n_exp - 1).astype(jnp.int32)
    del tile_start
    xs = _moe_dispatch(h_slab, pos, n_rows, d // LANES)
    hmid = _matmul(xs, w_gate, w2=w_up, tm=te, tn=_pick(dexp, (256, 128)), tiles=(tile_exp, tile_row),
                   slab_in=True, n_rows=n_rows, out_dtype=BF16, name="moe_up")
    y_slab = _matmul(hmid, w_down, tm=te, tn=d, tk=_pick(dexp, (512, 256, 128)),
                     tiles=(tile_exp, tile_row), slab_out=True, n_rows=n_rows, name="moe_down")
    return _moe_combine(tok, x, mods, mod_base, info, y_slab, pos, _pick(tok.row_tile(), (256, 128, 64, 32, 16, 8)))


def kernel(x_prompt, x_sample, c, cache_mla_ckv, cache_mla_kpe, state_ssd_fwd, state_ssd_bwd, c_ctx, norm1_g, norm2_g, w_mod, b_mod, norm_f_g, hy_w_in, hy_conv_w, hy_conv_b, hy_f_w1, hy_f_b1, hy_f_w2, hy_f_b2, hy_f_w3, hy_decay, hy_skip, hy_w_out, ssd_w_in, ssd_conv_w, ssd_conv_b, ssd_dt_bias, ssd_a_log, ssd_d, ssd_norm_g, ssd_w_out, mla_w_dq, mla_q_norm_g, mla_w_uq, mla_w_dkv, mla_kv_norm_g, mla_w_ukv, mla_w_o, ffn_w_gate, ffn_w_up, ffn_w_down, moe_w_router, moe_b_router, moe_w_gate, moe_w_up, moe_w_down):
    bp, lp, d = x_prompt.shape
    bs, ls, _ = x_sample.shape
    depth = w_mod.shape[0]
    tok = _Tokens(bp, lp, bs, ls)
    x = jnp.concatenate([x_prompt.reshape(tok.tp, d), x_sample.reshape(tok.ts, d)], axis=0)

    gp = -(-tok.groups // 8) * 8
    cond = jnp.concatenate([c_ctx[None, :], c, jnp.zeros((gp - tok.groups, d), F32)], axis=0)
    mods = _modulation(cond, w_mod, b_mod).reshape(depth * gp * 6, 1, d)

    ckv_new, kpe_new, sf_new, sb_new = [], [], [], []
    for i in range(depth):
        base = i * gp * 6
        j, kind = i // N_MIXERS, i % N_MIXERS
        if kind == 0:
            x = _hyena(tok, x, mods, base, norm1_g[i], hy_w_in[j], hy_conv_w[j], hy_conv_b[j], hy_f_w1[j],
                       hy_f_b1[j], hy_f_w2[j], hy_f_b2[j], hy_f_w3[j], hy_decay[j], hy_skip[j], hy_w_out[j])
        elif kind == 1:
            x, (s_f, s_b) = _ssd(tok, x, mods, base, norm1_g[i], ssd_w_in[j], ssd_conv_w[j], ssd_conv_b[j],
                                 ssd_dt_bias[j], ssd_a_log[j], ssd_d[j], ssd_norm_g[j], ssd_w_out[j],
                                 state_ssd_fwd[:, j], state_ssd_bwd[:, j])
            sf_new.append(s_f)
            sb_new.append(s_b)
        else:
            x, ckv, kpe = _mla(tok, x, mods, base, norm1_g[i], mla_w_dq[j], mla_q_norm_g[j], mla_w_uq[j],
                               mla_w_dkv[j], mla_kv_norm_g[j], mla_w_ukv[j], mla_w_o[j],
                               cache_mla_ckv[:, j], cache_mla_kpe[:, j])
            ckv_new.append(ckv)
            kpe_new.append(kpe)
        k = i // 2
        if i % 2 == 0:
            x = _dense_ffn(tok, x, mods, base, norm2_g[i], ffn_w_gate[k], ffn_w_up[k], ffn_w_down[k])
        else:
            x = _moe_ffn(tok, x, mods, base, norm2_g[i], moe_w_router[k], moe_b_router[k], moe_w_gate[k],
                         moe_w_up[k], moe_w_down[k])
    y = _rmsnorm(x, norm_f_g, tok.row_tile(512))
    y_prompt = y[:tok.tp].reshape(bp, lp, d)
    y_sample = y[tok.tp:].reshape(bs, ls, d)
    return (y_prompt, y_sample, jnp.stack(ckv_new, axis=1), jnp.stack(kpe_new, axis=1),
            jnp.stack(sf_new, axis=1), jnp.stack(sb_new, axis=1))
```

```python
import functools
import math

import jax
import jax.numpy as jnp
from jax import lax
from jax.experimental import pallas as pl
from jax.experimental.pallas import tpu as pltpu

F32 = jnp.float32
BF16 = jnp.bfloat16
EPS = 1e-6
N_MIXERS = 3
MLA_HEADS = 16
GRID_W = 64
ROPE_BASE = 10000.0
HY_BANDS = 16
HY_SIN_FREQ = 1.0
SSD_CHUNK = 128
LANES = 128
VMEM_LIMIT_BYTES = 56 * 1024 * 1024
HIGHEST = lax.Precision.HIGHEST


def _cparams(n_axes):
    return pltpu.CompilerParams(dimension_semantics=("arbitrary",) * n_axes,
                                vmem_limit_bytes=VMEM_LIMIT_BYTES)


def _pick(n, candidates):
    for c in candidates:
        if n % c == 0:
            return c
    raise ValueError(f"no tile for {n} in {candidates}")


def _silu(x):
    return x * (1.0 / (1.0 + jnp.exp(-x)))


def _dot(a, b):
    return jnp.dot(a, b, preferred_element_type=F32)


def _dot_nt(a, b):
    return lax.dot_general(a, b, (((1,), (1,)), ((), ())), preferred_element_type=F32)


def _dot_tn(a, b):
    return lax.dot_general(a, b, (((0,), (0,)), ((), ())), preferred_element_type=F32)


class _Tokens:
    def __init__(self, bp, lp, bs, ls):
        self.bp, self.lp, self.bs, self.ls = bp, lp, bs, ls
        self.tp, self.ts = bp * lp, bs * ls
        self.t = self.tp + self.ts
        self.groups = 1 + bs
        assert self.tp % ls == 0 or self.tp % lp == 0

    def row_tile(self, cap=1024):
        return _pick(math.gcd(self.tp, self.ls), [c for c in (1024, 512, 256, 128, 64, 32, 16, 8) if c <= cap])

    def group_of(self, row_start):
        return jnp.where(row_start < self.tp, 0, 1 + (row_start - self.tp) // self.ls)


def _mm_body(*refs, pre, swiglu, epi, nk, nj, grouped, slab_in, slab_out, precise, tn):
    refs = list(refs)
    tile_row_ref = None
    if grouped:
        refs.pop(0)
        tile_row_ref = refs.pop(0)
    x_ref = refs.pop(0)
    g_ref = refs.pop(0) if pre else None
    sh_ref = sc_ref = None
    if pre == "normmod":
        sh_ref = refs.pop(0)
        sc_ref = refs.pop(0)
    w_ref = refs.pop(0)
    w2_ref = refs.pop(0) if swiglu else None
    res_ref = gate_ref = None
    if epi == "resgate":
        res_ref = refs.pop(0)
        gate_ref = refs.pop(0)
    o_ref = refs.pop(0)
    h_scr = refs.pop(0) if (pre or slab_in) else None
    acc_ref = refs.pop(0) if nk > 1 else None
    i = pl.program_id(0)
    j = pl.program_id(1)
    k = pl.program_id(2)

    def prologue():
        if slab_in:
            tm, kdim = h_scr.shape
            slab = kdim // LANES
            for s in range(slab):
                h_scr[:, s * LANES:(s + 1) * LANES] = x_ref[pl.ds(s, tm, stride=slab), :].astype(BF16)
        else:
            xf = x_ref[...].astype(F32)
            ms = jnp.mean(xf * xf, axis=-1, keepdims=True)
            y = (xf * lax.rsqrt(ms + EPS)) * g_ref[...]
            if pre == "normmod":
                y = y * (1.0 + sc_ref[0]) + sh_ref[0]
            h_scr[...] = y.astype(BF16)

    def product(wr):
        if precise:
            return jnp.dot(x_ref[...], wr[...], preferred_element_type=F32, precision=HIGHEST)
        xb = h_scr[...] if (pre or slab_in) else x_ref[...].astype(BF16)
        return _dot(xb, wr[...].astype(BF16))

    def store_slab(a, s0):
        slab = (nj * tn) // LANES
        tm = o_ref.shape[0] // slab
        for q in range(tn // LANES):
            o_ref[pl.ds(s0 + q, tm, stride=slab), :] = a[:, q * LANES:(q + 1) * LANES].astype(o_ref.dtype)

    def finish(a, a2):
        if swiglu:
            a = _silu(a) * a2
        if epi == "resgate":
            a = res_ref[...] + gate_ref[0] * a
        if slab_out:
            for jj in range(nj):
                pl.when(j == jj)(functools.partial(store_slab, a, jj * (tn // LANES)))
        else:
            o_ref[...] = a.astype(o_ref.dtype)

    def compute():
        if pre or slab_in:
            pl.when(j == 0)(prologue)
        a = product(w_ref)
        a2 = product(w2_ref) if swiglu else None
        if nk == 1:
            finish(a, a2)
        else:
            @pl.when(k == 0)
            def _():
                acc_ref[...] = a

            @pl.when((k > 0) & (k < nk - 1))
            def _():
                acc_ref[...] += a

            @pl.when(k == nk - 1)
            def _():
                finish(acc_ref[...] + a, None)

    if grouped:
        used = tile_row_ref[i] == i
        pl.when(used)(compute)

        @pl.when(jnp.logical_not(used))
        def _():
            o_ref[...] = jnp.zeros_like(o_ref)
    else:
        compute()


def _matmul(x, w, *, tm, tn, tk=None, w2=None, pre=None, g=None, mods=None, mod_base=None, tok=None,
            shift_idx=None, scale_idx=None, epi=None, res=None, gate_idx=None, out_dtype=F32,
            x_col0=0, n_rows=None, row0=0, w_col0=0, n_cols=None, w_lead=(), tiles=None, slab_in=False,
            slab_out=False, precise=False, name="mm"):
    grouped = tiles is not None
    kdim = w.shape[-2]
    n = (w.shape[-1] - w_col0) if n_cols is None else n_cols
    tk = kdim if tk is None else tk
    nk = kdim // tk
    assert kdim % tk == 0 and n % tn == 0 and w_col0 % tn == 0
    wcb0 = w_col0 // tn
    swiglu = w2 is not None
    assert nk == 1 or not (pre or swiglu)
    if slab_in:
        assert x.shape[1] == LANES and kdim % LANES == 0 and nk == 1 and not pre
        rows = x.shape[0] // (kdim // LANES) if n_rows is None else n_rows
    else:
        rows = (x.shape[0] - row0) if n_rows is None else n_rows
    assert rows % tm == 0 and row0 % tm == 0
    ni, nj = (tiles[0].shape[0] if grouped else rows // tm), n // tn
    rb0 = row0 // tm
    xcb0 = x_col0 // tk
    assert x_col0 % tk == 0

    def rowblk(i, pref):
        return pref[1][i] if grouped else i + rb0

    def colblk(i, j, pref):
        return jnp.where(pref[1][i] == i, j, nj - 1) if grouped else j

    def kblk(i, k, pref):
        return jnp.where(pref[1][i] == i, k, nk - 1) if grouped else k

    def grp(i):
        return tok.group_of(i * tm)

    in_specs, args = [], []
    once = dict(pipeline_mode=pl.Buffered(1)) if (pre or slab_in) else {}
    if slab_in:
        in_specs.append(pl.BlockSpec((tm * (kdim // LANES), LANES), lambda i, j, k, *p: (rowblk(i, p), 0), **once))
    else:
        in_specs.append(pl.BlockSpec((tm, tk), lambda i, j, k, *p: (rowblk(i, p), xcb0 + kblk(i, k, p)), **once))
    args.append(x)
    if pre:
        in_specs.append(pl.BlockSpec((1, tk), lambda i, j, k, *p: (0, 0)))
        args.append(g.reshape(1, tk).astype(F32))
    if pre == "normmod":
        for idx in (shift_idx, scale_idx):
            in_specs.append(pl.BlockSpec((1, 1, tk), lambda i, j, k, *p, idx=idx: (mod_base + grp(i) * 6 + idx, 0, 0)))
            args.append(mods)
    lead = tuple(w_lead)
    none = (None,) * len(lead)
    wspec = (pl.BlockSpec(none + (None, tk, tn),
                          lambda i, j, k, *p: lead + (p[0][i], kblk(i, k, p), wcb0 + colblk(i, j, p))) if grouped
             else pl.BlockSpec(none + (tk, tn), lambda i, j, k, *p: lead + (k, wcb0 + j)))
    in_specs.append(wspec)
    args.append(w)
    if swiglu:
        in_specs.append(wspec)
        args.append(w2)
    if epi == "resgate":
        in_specs.append(pl.BlockSpec((tm, tn), lambda i, j, k, *p: (i + rb0, j)))
        args.append(res)
        in_specs.append(pl.BlockSpec((1, 1, tn), lambda i, j, k, *p: (mod_base + grp(i) * 6 + gate_idx, 0, j)))
        args.append(mods)
    if slab_out:
        assert tn % LANES == 0
        out_shape = jax.ShapeDtypeStruct((rows * (n // LANES), LANES), out_dtype)
        out_spec = pl.BlockSpec((tm * (n // LANES), LANES), lambda i, j, k, *p: (i, 0))
    else:
        out_shape = jax.ShapeDtypeStruct((rows, n), out_dtype)
        out_spec = pl.BlockSpec((tm, tn), lambda i, j, k, *p: (i, j))
    scratch = []
    if pre or slab_in:
        scratch.append(pltpu.VMEM((tm, tk), BF16))
    if nk > 1:
        scratch.append(pltpu.VMEM((tm, tn), F32))
    body = functools.partial(_mm_body, pre=pre, swiglu=swiglu, epi=epi, nk=nk, nj=nj, grouped=grouped,
                             slab_in=slab_in, slab_out=slab_out, precise=precise, tn=tn)
    grid_spec = pltpu.PrefetchScalarGridSpec(
        num_scalar_prefetch=2 if grouped else 0, grid=(ni, nj, nk),
        in_specs=in_specs, out_specs=out_spec, scratch_shapes=scratch)
    call = pl.pallas_call(body, grid_spec=grid_spec, out_shape=out_shape,
                          compiler_params=_cparams(3), name=name)
    return call(*tiles, *args) if grouped else call(*args)


def _mod_body(c_ref, w_ref, b_ref, o_ref):
    cb = _silu(c_ref[...]).astype(BF16)
    o_ref[...] = _dot(cb, w_ref[...].astype(BF16)) + b_ref[...]


def _modulation(cond, w_mod, b_mod):
    depth, d, n = w_mod.shape
    gp = cond.shape[0]
    tn = _pick(n, (1024, 512, 256, 128))
    return pl.pallas_call(
        _mod_body,
        grid=(depth, n // tn),
        in_specs=[pl.BlockSpec((gp, d), lambda i, j: (0, 0)),
                  pl.BlockSpec((None, d, tn), lambda i, j: (i, 0, j)),
                  pl.BlockSpec((None, 1, tn), lambda i, j: (i, 0, j))],
        out_specs=pl.BlockSpec((None, gp, tn), lambda i, j: (i, 0, j)),
        out_shape=jax.ShapeDtypeStruct((depth, gp, n), F32),
        compiler_params=_cparams(2), name="modulation",
    )(cond, w_mod, b_mod.reshape(depth, 1, n))


def _rms_body(x_ref, g_ref, o_ref):
    xf = x_ref[...]
    ms = jnp.mean(xf * xf, axis=-1, keepdims=True)
    o_ref[...] = (xf * lax.rsqrt(ms + EPS)) * g_ref[...]


def _rmsnorm(x, g, tm):
    t, d = x.shape
    return pl.pallas_call(
        _rms_body, grid=(t // tm,),
        in_specs=[pl.BlockSpec((tm, d), lambda i: (i, 0)), pl.BlockSpec((1, d), lambda i: (0, 0))],
        out_specs=pl.BlockSpec((tm, d), lambda i: (i, 0)),
        out_shape=jax.ShapeDtypeStruct((t, d), F32),
        compiler_params=_cparams(1), name="final_norm",
    )(x, g.reshape(1, d))


def _dft_tables(l):
    f = jnp.arange(l, dtype=jnp.int32)
    m = (f[:, None] * f[None, :]) % (2 * l)
    ang = m.astype(F32) * (math.pi / l)
    return jnp.cos(ang), jnp.sin(ang)


def _hy_features(l):
    t = jnp.arange(l, dtype=F32)
    t01 = t / (l - 1)
    bands = jnp.linspace(1e-4, HY_BANDS - 1, HY_BANDS, dtype=F32)
    ang = (2.0 * math.pi / l) * t[:, None] * bands[None, :]
    z = jnp.concatenate([t01[:, None], jnp.cos(ang), jnp.sin(ang)], axis=-1)
    return jnp.pad(z, ((0, 0), (0, 128 - z.shape[1])))


def _hy_taps_body(z_ref, w1_ref, b1_ref, w2_ref, b2_ref, w3f_ref, w3b_ref, dec_ref, ksum_ref, kdif_ref, knyq_ref):
    l = z_ref.shape[0]
    dot = functools.partial(jnp.dot, preferred_element_type=F32, precision=HIGHEST)
    f = jnp.sin(HY_SIN_FREQ * (dot(z_ref[...], w1_ref[...]) + b1_ref[...]))
    f = jnp.sin(HY_SIN_FREQ * (dot(f, w2_ref[...]) + b2_ref[...]))
    row = lax.broadcasted_iota(jnp.int32, (l, 1), 0)
    t01 = row.astype(F32) / (l - 1)
    kf = dot(f, w3f_ref[...]) * jnp.exp(-t01 * dec_ref[0:1, :])
    kb = dot(f, w3b_ref[...]) * jnp.exp(-t01 * dec_ref[1:2, :])
    kb = jnp.where(row == 0, 0.0, kb)
    l1 = jnp.sum(jnp.abs(kf), axis=0, keepdims=True) + jnp.sum(jnp.abs(kb), axis=0, keepdims=True)
    kf = kf / l1
    kb = kb / l1
    ks = kf + kb
    alt = jnp.where(row % 2 == 0, 1.0, -1.0)
    ksum_ref[...] = ks
    kdif_ref[...] = kb - kf
    knyq_ref[...] = jnp.sum(ks * alt, axis=0, keepdims=True)


def _hy_taps(l, w1, b1, w2, b2, w3, decay):
    d = decay.shape[1]
    nf = w2.shape[0]
    cb = _pick(d, (512, 256, 128))
    z = _hy_features(l)
    w1p = jnp.pad(w1, ((0, 128 - w1.shape[0]), (0, 0)))
    full = lambda shape: pl.BlockSpec(shape, lambda c: (0,) * len(shape))
    return pl.pallas_call(
        _hy_taps_body, grid=(d // cb,),
        in_specs=[full((l, 128)), full((128, nf)), full((1, nf)), full((nf, nf)), full((1, nf)),
                  pl.BlockSpec((nf, cb), lambda c: (0, c)),
                  pl.BlockSpec((nf, cb), lambda c: (0, d // cb + c)),
                  pl.BlockSpec((2, cb), lambda c: (0, c))],
        out_specs=[pl.BlockSpec((l, cb), lambda c: (0, c)), pl.BlockSpec((l, cb), lambda c: (0, c)),
                   pl.BlockSpec((1, cb), lambda c: (0, c))],
        out_shape=[jax.ShapeDtypeStruct((l, d), F32), jax.ShapeDtypeStruct((l, d), F32),
                   jax.ShapeDtypeStruct((1, d), F32)],
        compiler_params=_cparams(1), name="hyena_taps",
    )(z, w1p, b1.reshape(1, nf), w2, b2.reshape(1, nf), w3, w3, decay)


def _dwconv3(a, w_ref, b_ref, row, l):
    prev = jnp.where(row == 0, 0.0, pltpu.roll(a, 1, axis=0))
    nxt = jnp.where(row == l - 1, 0.0, pltpu.roll(a, l - 1, axis=0))
    return prev * w_ref[0:1, :] + a * w_ref[1:2, :] + nxt * w_ref[2:3, :] + b_ref[...]


def _hy_core_body(x0_ref, x1_ref, v_ref, w0_ref, w1_ref, wv_ref, b0_ref, b1_ref, bv_ref, skip_ref,
                  kr_ref, ki_ref, knyq_ref, c_ref, s_ref, o_ref):
    l = x0_ref.shape[0]
    row = lax.broadcasted_iota(jnp.int32, (l, 1), 0)
    x0 = _dwconv3(x0_ref[...], w0_ref, b0_ref, row, l)
    x1 = _dwconv3(x1_ref[...], w1_ref, b1_ref, row, l)
    v = _dwconv3(v_ref[...], wv_ref, bv_ref, row, l)
    z = v * x1
    zb = z.astype(BF16)
    cm = c_ref[...]
    sm = s_ref[...]
    zr = _dot(cm, zb)
    zs = _dot(sm, zb)
    wf = jnp.where(row == 0, 0.5 / l, 1.0 / l)
    kr = kr_ref[...] * wf
    ki = ki_ref[...] * wf
    a = (zr * kr + zs * ki).astype(BF16)
    b = (zs * kr - zr * ki).astype(BF16)
    alt = jnp.where(row % 2 == 0, 1.0, -1.0)
    znyq = jnp.sum(z * alt, axis=0, keepdims=True)
    y = _dot(cm, a) + _dot(sm, b) + alt * (znyq * knyq_ref[...] * (0.5 / l))
    o_ref[...] = (x0 * (y + z * skip_ref[...])).astype(o_ref.dtype)


def _hy_core(u, row0, nseq, l, d, conv_w, conv_b, skip, kr, ki, knyq, cmat, smat):
    cb = _pick(d, (128,) if l > 512 else (512, 256, 128))
    nd = d // cb
    sb0 = row0 // l
    assert row0 % l == 0
    useq = lambda sec: pl.BlockSpec((l, cb), lambda s, c, sec=sec: (sb0 + s, sec * nd + c))
    wsec = lambda sec: pl.BlockSpec((3, cb), lambda s, c, sec=sec: (0, sec * nd + c))
    bsec = lambda sec: pl.BlockSpec((1, cb), lambda s, c, sec=sec: (0, sec * nd + c))
    col = lambda rows: pl.BlockSpec((rows, cb), lambda s, c: (0, c))
    const = pl.BlockSpec((l, l), lambda s, c: (0, 0), pipeline_mode=pl.Buffered(1))
    cb3 = conv_b.reshape(1, 3 * d)
    return pl.pallas_call(
        _hy_core_body, grid=(nseq, nd),
        in_specs=[useq(0), useq(1), useq(2), wsec(0), wsec(1), wsec(2), bsec(0), bsec(1), bsec(2),
                  col(1), col(l), col(l), col(1), const, const],
        out_specs=pl.BlockSpec((l, cb), lambda s, c: (s, c)),
        out_shape=jax.ShapeDtypeStruct((nseq * l, d), BF16),
        compiler_params=_cparams(2), name=f"hyena_core_{l}",
    )(u, u, u, conv_w, conv_w, conv_w, cb3, cb3, cb3, skip.reshape(1, d), kr, ki, knyq, cmat, smat)


def _hyena(tok, x, mods, mod_base, g, lyr, w_in, conv_w, conv_b, f_w1, f_b1, f_w2, f_b2, f_w3, decay, skip, w_out):
    d = x.shape[1]
    tm = tok.row_tile()
    u = _matmul(x, w_in, w_lead=(lyr,), tm=tm, tn=_pick(3 * d, (512, 256, 128)), pre="normmod", g=g, mods=mods,
                mod_base=mod_base, tok=tok, shift_idx=0, scale_idx=1, name="hyena_in")
    parts = []
    for row0, nseq, l in ((0, tok.bp, tok.lp), (tok.tp, tok.bs, tok.ls)):
        ksum, kdif, knyq = _hy_taps(l, f_w1, f_b1, f_w2, f_b2, f_w3, decay)
        cmat, smat = _dft_tables(l)
        tf = _pick(l, (512, 256, 128))
        tn = _pick(d, (512, 256, 128))
        kr = _matmul(cmat, ksum, tm=tf, tn=tn, precise=True, name="hyena_spec_re")
        ki = _matmul(smat, kdif, tm=tf, tn=tn, precise=True, name="hyena_spec_im")
        parts.append(_hy_core(u, row0, nseq, l, d, conv_w, conv_b, skip, kr, ki, knyq,
                              cmat.astype(BF16), smat.astype(BF16)))
    gmix = jnp.concatenate(parts, axis=0)
    return _matmul(gmix, w_out, w_lead=(lyr,), tm=tm, tn=_pick(d, (512, 256, 128)), epi="resgate", res=x,
                   mods=mods, mod_base=mod_base, tok=tok, gate_idx=2, name="hyena_out")


def _ssd_conv_body(x_ref, w_ref, b_ref, o_ref):
    l = x_ref.shape[0]
    row = lax.broadcasted_iota(jnp.int32, (l, 1), 0)
    o_ref[...] = _silu(_dwconv3(x_ref[...], w_ref, b_ref, row, l))


def _ssd_conv(proj, col0, conv_w, conv_b, row0, nseq, l):
    cdim = conv_w.shape[1]
    cb = _pick(math.gcd(cdim, col0), (512, 256, 128))
    cb0, sb0 = col0 // cb, row0 // l
    return pl.pallas_call(
        _ssd_conv_body, grid=(nseq, cdim // cb),
        in_specs=[pl.BlockSpec((l, cb), lambda s, c: (sb0 + s, cb0 + c)),
                  pl.BlockSpec((3, cb), lambda s, c: (0, c)),
                  pl.BlockSpec((1, cb), lambda s, c: (0, c))],
        out_specs=pl.BlockSpec((l, cb), lambda s, c: (s, c)),
        out_shape=jax.ShapeDtypeStruct((nseq * l, cdim), F32),
        compiler_params=_cparams(2), name=f"ssd_conv_{l}",
    )(proj, conv_w, conv_b.reshape(1, cdim))


def _ssd_dt_body(raw_ref, bias_ref, alog_ref, dt_ref, cum_ref):
    tm, w = raw_ref.shape
    xr = raw_ref[...] + bias_ref[...]
    dt = jnp.maximum(xr, 0.0) + jnp.log1p(jnp.exp(-jnp.abs(xr)))
    dt_ref[...] = dt
    a = dt * (-jnp.exp(alog_ref[...]))
    ri = lax.broadcasted_iota(jnp.int32, (SSD_CHUNK, SSD_CHUNK), 0)
    ci = lax.broadcasted_iota(jnp.int32, (SSD_CHUNK, SSD_CHUNK), 1)
    lower = (ci <= ri).astype(F32)
    upper = (ci >= ri).astype(F32)
    lane = lax.broadcasted_iota(jnp.int32, (SSD_CHUNK, w), 1)
    for q in range(tm // SSD_CHUNK):
        ac = a[q * SSD_CHUNK:(q + 1) * SSD_CHUNK, :]
        pc = jnp.dot(lower, ac, preferred_element_type=F32, precision=HIGHEST)
        rc = jnp.dot(upper, ac, preferred_element_type=F32, precision=HIGHEST)
        cum_ref[q * SSD_CHUNK:(q + 1) * SSD_CHUNK, :] = jnp.where(lane < w // 2, pc, rc)


def _ssd_dt(proj, col0, dt_bias, a_log, tm):
    t = proj.shape[0]
    w = dt_bias.size
    assert col0 % w == 0
    return pl.pallas_call(
        _ssd_dt_body, grid=(t // tm,),
        in_specs=[pl.BlockSpec((tm, w), lambda i: (i, col0 // w)),
                  pl.BlockSpec((1, w), lambda i: (0, 0)), pl.BlockSpec((1, w), lambda i: (0, 0))],
        out_specs=[pl.BlockSpec((tm, w), lambda i: (i, 0)), pl.BlockSpec((tm, w), lambda i: (i, 0))],
        out_shape=[jax.ShapeDtypeStruct((t, w), F32), jax.ShapeDtypeStruct((t, w), F32)],
        compiler_params=_cparams(1), name="ssd_dt",
    )(proj, dt_bias.reshape(1, w), a_log.reshape(1, w))


def _ssd_scan_body(*refs, has_init, r, p, nc):
    refs = list(refs)
    x_ref, b_ref, c_ref, dsk_ref = refs[:4]
    dt_refs, cum_refs, cumt_refs = refs[4:6], refs[6:8], refs[8:10]
    refs = refs[10:]
    init_refs = (refs.pop(0), refs.pop(0)) if has_init else None
    y_ref, fin_refs, st_ref = refs[0], refs[1:3], refs[3]
    ch = SSD_CHUNK
    npair = r // 2
    lane = lax.broadcasted_iota(jnp.int32, (ch, 2 * p), 1)
    left = lane < p
    ri = lax.broadcasted_iota(jnp.int32, (ch, ch), 0)
    ci = lax.broadcasted_iota(jnp.int32, (ch, ch), 1)

    for d in range(2):
        keep = (ci <= ri) if d == 0 else (ci >= ri)
        for pr in range(npair):
            if has_init:
                st_ref[pr] = init_refs[d][pr * 2 * p:(pr + 1) * 2 * p, :].T
            else:
                st_ref[pr] = jnp.zeros((st_ref.shape[1], 2 * p), F32)

        def chunk(ic, carry, d=d, keep=keep):
            cidx = ic if d == 0 else nc - 1 - ic
            r0 = pl.multiple_of(cidx * ch, ch)
            rows = pl.ds(r0, ch)
            bc = b_ref[rows, :].astype(BF16)
            cc = c_ref[rows, :].astype(BF16)
            cbm = _dot_nt(cc, bc)
            bt = b_ref[rows, :].T.astype(BF16)
            dtc = dt_refs[d][rows, :]
            cumc = cum_refs[d][rows, :]
            cumt = cumt_refs[d][:, rows]
            edge = cumc[0:1, :] if d == 1 else cumc[ch - 1:ch, :]
            for pr in range(npair):
                xpair = x_ref[rows, pr * 2 * p:(pr + 1) * 2 * p]
                ha, hb = 2 * pr, 2 * pr + 1
                sel = lambda col: jnp.where(left, col[:, ha:ha + 1], col[:, hb:hb + 1])
                xs = xpair * sel(dtc)
                xsb = xs.astype(BF16)
                ydiag = []
                for h in (ha, hb):
                    seg = cumc[:, h:h + 1] - cumt[h:h + 1, :]
                    dec = jnp.exp(jnp.where(keep, seg, -jnp.inf))
                    ydiag.append(_dot((cbm * dec).astype(BF16), xsb))
                cum_pair = sel(cumc)
                st = st_ref[pr]
                yoff = _dot(cc, st.astype(BF16)) * jnp.exp(cum_pair)
                yc = jnp.where(left, ydiag[0], ydiag[1]) + yoff
                edge_pair = jnp.where(left[0:1, :], edge[:, ha:ha + 1], edge[:, hb:hb + 1])
                xsd = (xs * jnp.exp(edge_pair - cum_pair)).astype(BF16)
                st_ref[pr] = st * jnp.exp(edge_pair) + _dot(bt, xsd)
                cols = slice(pr * 2 * p, (pr + 1) * 2 * p)
                if d == 0:
                    y_ref[rows, cols] = yc + xpair * dsk_ref[:, cols]
                else:
                    y_ref[rows, cols] += yc
            return carry

        lax.fori_loop(0, nc, chunk, 0)
        for pr in range(npair):
            fin_refs[d][pr * 2 * p:(pr + 1) * 2 * p, :] = st_ref[pr].T


def _ssd_scan(xbc, dtg, cumg, cumtg, d_vec, nseq, l, di, g, n, p, init):
    h = di // p
    r = h // g
    rp = r * p
    assert r % 2 == 0 and rp % 128 == 0 and n % 128 == 0 and l % SSD_CHUNK == 0
    nb0, nc0 = di // n, (di + g * n) // n
    has_init = init is not None
    dspec = lambda d: pl.BlockSpec((None, l, r), lambda s, gi, d=d: (d * g + gi, s, 0))
    tspec = lambda d: pl.BlockSpec((None, r, l), lambda s, gi, d=d: (d * g + gi, 0, s))
    sspec = pl.BlockSpec((None, rp, n), lambda s, gi: (s, gi, 0))
    in_specs = [pl.BlockSpec((l, rp), lambda s, gi: (s, gi)),
                pl.BlockSpec((l, n), lambda s, gi: (s, nb0 + gi)),
                pl.BlockSpec((l, n), lambda s, gi: (s, nc0 + gi)),
                pl.BlockSpec((1, rp), lambda s, gi: (0, gi)),
                dspec(0), dspec(1), dspec(0), dspec(1), tspec(0), tspec(1)]
    args = [xbc, xbc, xbc, d_vec, dtg, dtg, cumg, cumg, cumtg, cumtg]
    if has_init:
        in_specs += [sspec, sspec]
        args += [s.reshape(nseq, h * p, n) for s in init]
    st_shape = jax.ShapeDtypeStruct((nseq, h * p, n), F32)
    body = functools.partial(_ssd_scan_body, has_init=has_init, r=r, p=p, nc=l // SSD_CHUNK)
    y, sf, sb = pl.pallas_call(
        body, grid=(nseq, g), in_specs=in_specs,
        out_specs=[pl.BlockSpec((l, rp), lambda s, gi: (s, gi)), sspec, sspec],
        out_shape=[jax.ShapeDtypeStruct((nseq * l, di), F32), st_shape, st_shape],
        scratch_shapes=[pltpu.VMEM((r // 2, n, 2 * p), F32)],
        compiler_params=_cparams(2), name=f"ssd_scan_{l}",
    )(*args)
    return y, sf.reshape(nseq, h, p, n), sb.reshape(nseq, h, p, n)


def _ssd_gate_body(y_ref, z_ref, g_ref, o_ref):
    v = y_ref[...] * _silu(z_ref[...])
    ms = jnp.mean(v * v, axis=-1, keepdims=True)
    o_ref[...] = ((v * lax.rsqrt(ms + EPS)) * g_ref[...]).astype(o_ref.dtype)


def _ssd_gate(y, proj, norm_g, tm):
    t, di = y.shape
    return pl.pallas_call(
        _ssd_gate_body, grid=(t // tm,),
        in_specs=[pl.BlockSpec((tm, di), lambda i: (i, 0)), pl.BlockSpec((tm, di), lambda i: (i, 0)),
                  pl.BlockSpec((1, di), lambda i: (0, 0))],
        out_specs=pl.BlockSpec((tm, di), lambda i: (i, 0)),
        out_shape=jax.ShapeDtypeStruct((t, di), BF16),
        compiler_params=_cparams(1), name="ssd_gate",
    )(y, proj, norm_g.reshape(1, di))


def _ssd(tok, x, mods, mod_base, gnorm, w_in, conv_w, conv_b, dt_bias, a_log, d_skip, norm_g, w_out,
         init_f, init_b):
    d = x.shape[1]
    p, n = init_f.shape[-2], init_f.shape[-1]
    h = dt_bias.shape[1]
    di = h * p
    cdim = conv_w.shape[1]
    g = (cdim - di) // (2 * n)
    r = h // g
    tm = tok.row_tile()
    pre_kw = dict(pre="normmod", g=gnorm, mods=mods, mod_base=mod_base, tok=tok, shift_idx=0, scale_idx=1)
    proj = _matmul(x, w_in, tm=tm, tn=_pick(math.gcd(di, cdim), (512, 256, 128)), n_cols=di + cdim,
                   name="ssd_in", **pre_kw)
    dt_raw = _matmul(x, w_in, tm=tm, tn=2 * h, w_col0=di + cdim, name="ssd_in_dt", **pre_kw)
    dt, cum = _ssd_dt(dt_raw, 0, dt_bias, a_log, _pick(tok.t, (512, 256, 128)))
    split = lambda a: jnp.transpose(a.reshape(tok.t, 2 * g, r), (1, 0, 2))
    dtg, cumg = split(dt), split(cum)
    cumtg = jnp.transpose(cumg, (0, 2, 1))
    d_vec = jnp.repeat(d_skip.astype(F32), p).reshape(1, di)
    ys, states = [], []
    for row0, nseq, l, init in ((0, tok.bp, tok.lp, None), (tok.tp, tok.bs, tok.ls, (init_f, init_b))):
        xbc = _ssd_conv(proj, di, conv_w, conv_b, row0, nseq, l)
        sl = slice(row0, row0 + nseq * l)
        y, sf, sb = _ssd_scan(xbc, dtg[:, sl], cumg[:, sl], cumtg[:, :, sl], d_vec, nseq, l, di, g, n, p, init)
        ys.append(y)
        states.append((sf, sb))
    yg = _ssd_gate(jnp.concatenate(ys, axis=0), proj, norm_g, _pick(tok.t, (256, 128)))
    out = _matmul(yg, w_out, tm=tm, tn=_pick(d, (512, 256, 128)), tk=_pick(di, (2048, 1024, 512, 256)),
                  epi="resgate", res=x, mods=mods, mod_base=mod_base, tok=tok, gate_idx=2, name="ssd_out")
    return out, states[0]


def _rope_tables(n_tokens, dr):
    axis_dim = dr // 2
    rows = n_tokens // GRID_W
    row = jnp.broadcast_to(jnp.arange(rows, dtype=F32)[:, None], (rows, GRID_W)).reshape(-1)
    col = jnp.broadcast_to(jnp.arange(GRID_W, dtype=F32)[None, :], (rows, GRID_W)).reshape(-1)
    inv_freq = ROPE_BASE ** (-jnp.arange(0, axis_dim, 2, dtype=F32) / axis_dim)
    ang = jnp.concatenate([row[:, None] * inv_freq, col[:, None] * inv_freq], axis=-1)
    return jnp.cos(ang), jnp.sin(ang)


def _rot_cols(w, dr):
    half = dr // 2
    return jnp.concatenate([-w[..., half:], w[..., :half]], axis=-1)


def _mla_kv_body(kv_ref, pe_ref, g_ref, cs_ref, ckv_ref, kpe_ref):
    kv = kv_ref[...]
    ms = jnp.mean(kv * kv, axis=-1, keepdims=True)
    ckv_ref[...] = (kv * lax.rsqrt(ms + EPS)) * g_ref[...]
    prod = pe_ref[...] * cs_ref[...]
    kpe_ref[...] = prod + pltpu.roll(prod, prod.shape[1] // 2, axis=1)


def _mla_kv(qkv, col0, rank, dr, g, cs, tm):
    t = qkv.shape[0]
    assert col0 % rank == 0 and (col0 + rank) % (2 * dr) == 0 and 2 * dr == 128
    return pl.pallas_call(
        _mla_kv_body, grid=(t // tm,),
        in_specs=[pl.BlockSpec((tm, rank), lambda i: (i, col0 // rank)),
                  pl.BlockSpec((tm, 2 * dr), lambda i: (i, (col0 + rank) // (2 * dr))),
                  pl.BlockSpec((1, rank), lambda i: (0, 0)),
                  pl.BlockSpec((tm, 2 * dr), lambda i: (i, 0))],
        out_specs=[pl.BlockSpec((tm, rank), lambda i: (i, 0)), pl.BlockSpec((tm, 2 * dr), lambda i: (i, 0))],
        out_shape=[jax.ShapeDtypeStruct((t, rank), F32), jax.ShapeDtypeStruct((t, 2 * dr), F32)],
        compiler_params=_cparams(1), name="mla_kv",
    )(qkv, qkv, g.reshape(1, rank), cs)


def _attn_body(qn_ref, qp_ref, qr_ref, cos_ref, sin_ref, kv_ref, kpe_ref, o_ref, *, dn, dr, dv, scale):
    qrot = qp_ref[...] * cos_ref[...] + qr_ref[...] * sin_ref[...]
    lane = lax.broadcasted_iota(jnp.int32, qrot.shape, 1)
    kpe = kpe_ref[...].astype(BF16)
    for h in range(2):
        qn = qn_ref[:, h * dn:(h + 1) * dn].astype(BF16)
        kn = kv_ref[:, h * (dn + dv):h * (dn + dv) + dn]
        v = kv_ref[:, h * (dn + dv) + dn:(h + 1) * (dn + dv)]
        qh = jnp.where((lane >= h * dr) & (lane < (h + 1) * dr), qrot, 0.0).astype(BF16)
        s = (_dot_nt(qn, kn) + _dot_nt(qh, kpe)) * scale
        m = jnp.max(s, axis=-1, keepdims=True)
        e = jnp.exp(s - m)
        pn = e / jnp.sum(e, axis=-1, keepdims=True)
        o_ref[:, h * dv:(h + 1) * dv] = _dot(pn.astype(BF16), v).astype(o_ref.dtype)


def _attention(q, cosq, sinq, kv, kpe, row0, nb, lq, lk, heads, dn, dr, dv):
    assert dn == 128 and dv == 128 and dr == 64 and heads % 2 == 0
    tq = _pick(lq, (512, 256, 128))
    nq = lq // tq
    rb0 = row0 // tq
    hp = heads // 2
    pe0, rot0 = heads * dn // 128, (heads * dn + heads * dr) // 128
    body = functools.partial(_attn_body, dn=dn, dr=dr, dv=dv, scale=(dn + dr) ** -0.5)
    return pl.pallas_call(
        body, grid=(nb, hp, nq),
        in_specs=[pl.BlockSpec((tq, 2 * dn), lambda b, h, i: (rb0 + b * nq + i, h)),
                  pl.BlockSpec((tq, 128), lambda b, h, i: (rb0 + b * nq + i, pe0 + h)),
                  pl.BlockSpec((tq, 128), lambda b, h, i: (rb0 + b * nq + i, rot0 + h)),
                  pl.BlockSpec((tq, 128), lambda b, h, i: (rb0 + b * nq + i, 0)),
                  pl.BlockSpec((tq, 128), lambda b, h, i: (rb0 + b * nq + i, 0)),
                  pl.BlockSpec((None, lk, 2 * (dn + dv)), lambda b, h, i: (b, 0, h)),
                  pl.BlockSpec((None, lk, 128), lambda b, h, i: (b, 0, 0))],
        out_specs=pl.BlockSpec((tq, 2 * dv), lambda b, h, i: (b * nq + i, h)),
        out_shape=jax.ShapeDtypeStruct((nb * lq, heads * dv), BF16),
        compiler_params=_cparams(3), name=f"mla_attn_{lq}",
    )(q, q, q, cosq, sinq, kv, kpe)


def _mla(tok, x, mods, mod_base, gnorm, w_dq, q_norm_g, w_uq, w_dkv, kv_norm_g, w_ukv, w_o, cache_ckv, cache_kpe):
    d = x.shape[1]
    heads = MLA_HEADS
    qr_, kvr, dr = w_dq.shape[1], kv_norm_g.shape[0], cache_kpe.shape[-1]
    dv = w_o.shape[0] // heads
    dn = w_ukv.shape[1] // heads - dv
    past = cache_ckv.shape[1]
    tm = tok.row_tile()
    w_a = jnp.concatenate([w_dq, w_dkv, _rot_cols(w_dkv[:, kvr:], dr)], axis=1)
    wq = w_uq.reshape(qr_, heads, dn + dr)
    wq_pe = wq[:, :, dn:]
    w_q = jnp.concatenate([wq[:, :, :dn].reshape(qr_, heads * dn), wq_pe.reshape(qr_, heads * dr),
                           _rot_cols(wq_pe, dr).reshape(qr_, heads * dr)], axis=1)
    qkv = _matmul(x, w_a, tm=tm, tn=_pick(w_a.shape[1], (384, 256, 128)), pre="normmod", g=gnorm, mods=mods,
                  mod_base=mod_base, tok=tok, shift_idx=0, scale_idx=1, name="mla_down")
    cos, sin = _rope_tables(tok.ls, dr)
    cos2 = jnp.tile(jnp.concatenate([cos, cos], axis=1), (tok.bs, 1))
    sin2 = jnp.tile(jnp.concatenate([sin, sin], axis=1), (tok.bs, 1))
    ones, zeros = jnp.ones((tok.tp, dr), F32), jnp.zeros((tok.tp, dr), F32)
    cos_t = jnp.concatenate([ones, cos2], axis=0)
    sin_t = jnp.concatenate([zeros, sin2], axis=0)
    ckv, kpe = _mla_kv(qkv, qr_, kvr, dr, kv_norm_g, jnp.concatenate([cos_t, sin_t], axis=1), tm)
    q = _matmul(qkv, w_q, tm=tm, tn=_pick(w_q.shape[1], (512, 256, 128)), tk=qr_, pre="norm", g=q_norm_g,
                name="mla_q")
    cosq = jnp.concatenate([cos_t, cos_t], axis=1)
    sinq = jnp.concatenate([sin_t, sin_t], axis=1)
    ckv_s = jnp.concatenate([cache_ckv, ckv[tok.tp:].reshape(tok.bs, tok.ls, kvr)], axis=1)
    kpe_s = jnp.concatenate([jnp.tile(cache_kpe, (1, 1, 2)), kpe[tok.tp:].reshape(tok.bs, tok.ls, 2 * dr)], axis=1)
    lk_s = past + tok.ls
    ckv_all = jnp.concatenate([ckv[:tok.tp], ckv_s.reshape(tok.bs * lk_s, kvr)], axis=0)
    kv = _matmul(ckv_all, w_ukv, tm=_pick(math.gcd(tok.tp, tok.bs * lk_s), (1024, 512, 256, 128)),
                 tn=_pick(w_ukv.shape[1], (512, 256, 128)), out_dtype=BF16, name="mla_up_kv")
    nkv = kv.shape[1]
    o_p = _attention(q, cosq, sinq, kv[:tok.tp].reshape(tok.bp, tok.lp, nkv),
                     kpe[:tok.tp].reshape(tok.bp, tok.lp, 2 * dr), 0, tok.bp, tok.lp, tok.lp, heads, dn, dr, dv)
    o_s = _attention(q, cosq, sinq, kv[tok.tp:].reshape(tok.bs, lk_s, nkv), kpe_s, tok.tp, tok.bs, tok.ls, lk_s,
                     heads, dn, dr, dv)
    o = jnp.concatenate([o_p, o_s], axis=0)
    out = _matmul(o, w_o, tm=tm, tn=_pick(d, (512, 256, 128)), epi="resgate", res=x, mods=mods,
                  mod_base=mod_base, tok=tok, gate_idx=2, name="mla_out")
    new_ckv = ckv[:tok.tp].reshape(tok.bp, tok.lp, kvr)
    new_kpe = qkv[:tok.tp, qr_ + kvr:qr_ + kvr + dr].reshape(tok.bp, tok.lp, dr)
    return out, new_ckv, new_kpe


def _k_tile(kdim, cap):
    return max(c for c in range(LANES, cap + 1, LANES) if kdim % c == 0)


def _dense_ffn(tok, x, mods, mod_base, gnorm, lyr, w_gate, w_up, w_down):
    _, d, dff = w_gate.shape
    tm = tok.row_tile()
    hmid = _matmul(x, w_gate, w2=w_up, w_lead=(lyr,), tm=tm, tn=_pick(dff, (512, 256, 128)), pre="normmod", g=gnorm,
                   mods=mods, mod_base=mod_base, tok=tok, shift_idx=3, scale_idx=4, out_dtype=BF16, name="ffn_up")
    return _matmul(hmid, w_down, w_lead=(lyr,), tm=tm, tn=_pick(d, (512, 256, 128)), tk=_k_tile(dff, 3072),
                   epi="resgate", res=x, mods=mods, mod_base=mod_base, tok=tok, gate_idx=5, name="ffn_down")


def _route_body(x_ref, g_ref, sh_ref, sc_ref, wr_ref, br_ref, h_ref, info_ref, cnt_ref, carry_ref, *, n_exp):
    i = pl.program_id(0)
    tm, d = x_ref.shape
    slab = d // LANES

    @pl.when(i == 0)
    def _():
        carry_ref[...] = jnp.zeros_like(carry_ref)

    xf = x_ref[...]
    ms = jnp.mean(xf * xf, axis=-1, keepdims=True)
    hh = ((xf * lax.rsqrt(ms + EPS)) * g_ref[...]) * (1.0 + sc_ref[0]) + sh_ref[0]
    for s in range(slab):
        h_ref[pl.ds(s, tm, stride=slab), :] = hh[:, s * LANES:(s + 1) * LANES]
    logits = jnp.dot(hh, wr_ref[...], preferred_element_type=F32, precision=HIGHEST) + br_ref[...]
    lane = lax.broadcasted_iota(jnp.int32, logits.shape, 1)
    logits = jnp.where(lane < n_exp, logits, -jnp.inf)
    t1 = jnp.max(logits, axis=-1, keepdims=True)
    i1 = jnp.min(jnp.where(logits == t1, lane, 128), axis=-1, keepdims=True)
    rest = jnp.where(lane == i1, -jnp.inf, logits)
    t2 = jnp.max(rest, axis=-1, keepdims=True)
    i2 = jnp.min(jnp.where(rest == t2, lane, 128), axis=-1, keepdims=True)
    e2 = jnp.exp(t2 - t1)
    g1 = 1.0 / (1.0 + e2)
    g2 = e2 / (1.0 + e2)
    oh1 = lane == i1
    oh2 = lane == i2
    cnt = jnp.where(oh1 | oh2, 1.0, 0.0).astype(BF16)
    ri = lax.broadcasted_iota(jnp.int32, (tm, tm), 0)
    ci = lax.broadcasted_iota(jnp.int32, (tm, tm), 1)
    before = _dot((ci < ri).astype(BF16), cnt) + carry_ref[...]
    r1 = jnp.sum(jnp.where(oh1, before, 0.0), axis=-1, keepdims=True)
    r2 = jnp.sum(jnp.where(oh2, before, 0.0), axis=-1, keepdims=True)
    carry_ref[...] += jnp.sum(cnt.astype(F32), axis=0, keepdims=True)
    vals = (i1.astype(F32), i2.astype(F32), r1, r2, g1, g2)
    info = jnp.zeros(logits.shape, F32)
    for q, v in enumerate(vals):
        info = jnp.where(lane == q, v, info)
    info_ref[...] = info
    cnt_ref[...] = carry_ref[...]


def _moe_route(tok, x, mods, mod_base, gnorm, w_router, b_router, tm):
    t, d = x.shape
    n_exp = w_router.shape[1]
    wr = jnp.pad(w_router, ((0, 0), (0, 128 - n_exp)))
    br = jnp.pad(b_router, (0, 128 - n_exp)).reshape(1, 128)
    slab = d // LANES
    modspec = lambda idx: pl.BlockSpec((1, 1, d), lambda i: (mod_base + tok.group_of(i * tm) * 6 + idx, 0, 0))
    return pl.pallas_call(
        functools.partial(_route_body, n_exp=n_exp), grid=(t // tm,),
        in_specs=[pl.BlockSpec((tm, d), lambda i: (i, 0)), pl.BlockSpec((1, d), lambda i: (0, 0)),
                  modspec(3), modspec(4),
                  pl.BlockSpec((d, 128), lambda i: (0, 0)), pl.BlockSpec((1, 128), lambda i: (0, 0))],
        out_specs=[pl.BlockSpec((tm * slab, LANES), lambda i: (i, 0)),
                   pl.BlockSpec((tm, 128), lambda i: (i, 0)),
                   pl.BlockSpec((1, 128), lambda i: (0, 0))],
        out_shape=[jax.ShapeDtypeStruct((t * slab, LANES), F32), jax.ShapeDtypeStruct((t, 128), F32),
                   jax.ShapeDtypeStruct((1, 128), F32)],
        scratch_shapes=[pltpu.VMEM((1, 128), F32)],
        compiler_params=_cparams(1), name="moe_route",
    )(x, gnorm.reshape(1, d), mods, mods, wr, br)


def _dispatch_body(dest_ref, h_ref, init_hbm, o_hbm, sem, *, tm, slab):
    del init_hbm
    base = pl.program_id(0) * tm

    def copies(r):
        src = h_ref.at[pl.ds(r * slab, slab)]
        return [pltpu.make_async_copy(src, o_hbm.at[pl.ds(dest_ref[2 * (base + r) + k] * slab, slab)], sem)
                for k in range(2)]

    def start(r, c):
        for cp in copies(r):
            cp.start()
        return c

    def wait(r, c):
        for cp in copies(r):
            cp.wait()
        return c

    lax.fori_loop(0, tm, start, 0)
    lax.fori_loop(0, tm, wait, 0)


def _moe_dispatch(h_slab, dest, n_rows, slab, tm):
    t = h_slab.shape[0] // slab
    init = jnp.zeros((n_rows * slab, LANES), F32)
    grid_spec = pltpu.PrefetchScalarGridSpec(
        num_scalar_prefetch=1, grid=(t // tm,),
        in_specs=[pl.BlockSpec((tm * slab, LANES), lambda i, p: (i, 0)), pl.BlockSpec(memory_space=pl.ANY)],
        out_specs=pl.BlockSpec(memory_space=pl.ANY),
        scratch_shapes=[pltpu.SemaphoreType.DMA(())])
    return pl.pallas_call(
        functools.partial(_dispatch_body, tm=tm, slab=slab), grid_spec=grid_spec,
        out_shape=jax.ShapeDtypeStruct(init.shape, F32), input_output_aliases={2: 0},
        compiler_params=_cparams(1), name="moe_dispatch",
    )(dest, h_slab, init)


def _combine_body(pos_ref, x_ref, gate_ref, info_ref, y_hbm, o_ref, buf1, buf2, sem, *, tm):
    base = pl.program_id(0) * tm
    slab = buf1.shape[0] // tm

    def copies(r):
        rows = pl.ds(r * slab, slab)
        src = lambda k: y_hbm.at[pl.ds(pos_ref[2 * (base + r) + k] * slab, slab)]
        return (pltpu.make_async_copy(src(0), buf1.at[rows], sem),
                pltpu.make_async_copy(src(1), buf2.at[rows], sem))

    def start(r, c):
        for cp in copies(r):
            cp.start()
        return c

    def wait(r, c):
        for cp in copies(r):
            cp.wait()
        return c

    lax.fori_loop(0, tm, start, 0)
    lax.fori_loop(0, tm, wait, 0)
    g1 = info_ref[:, 4:5]
    g2 = info_ref[:, 5:6]
    for s in range(slab):
        cols = slice(s * LANES, (s + 1) * LANES)
        rows = pl.ds(s, tm, stride=slab)
        mix = buf1[rows, :] * g1 + buf2[rows, :] * g2
        o_ref[:, cols] = x_ref[:, cols] + gate_ref[0][:, cols] * mix


def _moe_combine(tok, x, mods, mod_base, info, y_slab, pos, tm):
    t, d = x.shape
    slab = d // LANES
    grid_spec = pltpu.PrefetchScalarGridSpec(
        num_scalar_prefetch=1, grid=(t // tm,),
        in_specs=[pl.BlockSpec((tm, d), lambda i, p: (i, 0)),
                  pl.BlockSpec((1, 1, d), lambda i, p: (mod_base + tok.group_of(i * tm) * 6 + 5, 0, 0)),
                  pl.BlockSpec((tm, 128), lambda i, p: (i, 0)),
                  pl.BlockSpec(memory_space=pl.ANY)],
        out_specs=pl.BlockSpec((tm, d), lambda i, p: (i, 0)),
        scratch_shapes=[pltpu.VMEM((tm * slab, LANES), F32), pltpu.VMEM((tm * slab, LANES), F32),
                        pltpu.SemaphoreType.DMA(())])
    return pl.pallas_call(
        functools.partial(_combine_body, tm=tm), grid_spec=grid_spec,
        out_shape=jax.ShapeDtypeStruct((t, d), F32),
        compiler_params=_cparams(1), name="moe_combine",
    )(pos, x, mods, info, y_slab)


def _moe_ffn(tok, x, mods, mod_base, gnorm, lyr, w_router, b_router, w_gate, w_up, w_down):
    t, d = x.shape
    _, n_exp, _, dexp = w_gate.shape
    tm = tok.row_tile()
    h_slab, info, counts = _moe_route(tok, x, mods, mod_base, gnorm, w_router, b_router, tm)
    te = _pick(2 * t, (1024, 512, 256, 128, 64, 32, 16, 8))
    counts = counts[0, :n_exp].astype(jnp.int32)
    padded = (counts + te - 1) // te * te
    pad_end = jnp.cumsum(padded)
    pad_start = pad_end - padded
    idx = info[:, 0:2].astype(jnp.int32)
    rank = info[:, 2:4].astype(jnp.int32)
    pos = (pad_start[idx] + rank).reshape(-1)
    n_tiles = 2 * t // te + n_exp
    n_rows = n_tiles * te
    used = pad_end[-1] // te
    tile_row = jnp.minimum(jnp.arange(n_tiles, dtype=jnp.int32), used - 1)
    tile_exp = jnp.minimum(jnp.searchsorted(pad_end, tile_row * te, side="right"), n_exp - 1).astype(jnp.int32)
    tmc = _pick(tok.row_tile(), (256, 128, 64, 32, 16, 8))
    xs = _moe_dispatch(h_slab, pos, n_rows, d // LANES, tmc)
    hmid = _matmul(xs, w_gate, w2=w_up, w_lead=(lyr,), tm=te, tn=_pick(dexp, (512, 256, 128)),
                   tiles=(tile_exp, tile_row), slab_in=True, n_rows=n_rows, out_dtype=BF16, name="moe_up")
    y_slab = _matmul(hmid, w_down, w_lead=(lyr,), tm=te, tn=_pick(d, (512, 256, 128)), tk=_k_tile(dexp, 2048),
                     tiles=(tile_exp, tile_row), slab_out=True, n_rows=n_rows, name="moe_down")
    return _moe_combine(tok, x, mods, mod_base, info, y_slab, pos, tmc)


def kernel(x_prompt, x_sample, c, cache_mla_ckv, cache_mla_kpe, state_ssd_fwd, state_ssd_bwd, c_ctx, norm1_g, norm2_g, w_mod, b_mod, norm_f_g, hy_w_in, hy_conv_w, hy_conv_b, hy_f_w1, hy_f_b1, hy_f_w2, hy_f_b2, hy_f_w3, hy_decay, hy_skip, hy_w_out, ssd_w_in, ssd_conv_w, ssd_conv_b, ssd_dt_bias, ssd_a_log, ssd_d, ssd_norm_g, ssd_w_out, mla_w_dq, mla_q_norm_g, mla_w_uq, mla_w_dkv, mla_kv_norm_g, mla_w_ukv, mla_w_o, ffn_w_gate, ffn_w_up, ffn_w_down, moe_w_router, moe_b_router, moe_w_gate, moe_w_up, moe_w_down):
    bp, lp, d = x_prompt.shape
    bs, ls, _ = x_sample.shape
    depth = w_mod.shape[0]
    tok = _Tokens(bp, lp, bs, ls)
    x = jnp.concatenate([x_prompt.reshape(tok.tp, d), x_sample.reshape(tok.ts, d)], axis=0)

    gp = -(-tok.groups // 8) * 8
    cond = jnp.concatenate([c_ctx[None, :], c, jnp.zeros((gp - tok.groups, d), F32)], axis=0)
    mods = _modulation(cond, w_mod, b_mod).reshape(depth * gp * 6, 1, d)

    ckv_new, kpe_new, sf_new, sb_new = [], [], [], []
    for i in range(depth):
        base = i * gp * 6
        j, kind = i // N_MIXERS, i % N_MIXERS
        if kind == 0:
            x = _hyena(tok, x, mods, base, norm1_g[i], j, hy_w_in, hy_conv_w[j], hy_conv_b[j], hy_f_w1[j],
                       hy_f_b1[j], hy_f_w2[j], hy_f_b2[j], hy_f_w3[j], hy_decay[j], hy_skip[j], hy_w_out)
        elif kind == 1:
            x, (s_f, s_b) = _ssd(tok, x, mods, base, norm1_g[i], ssd_w_in[j], ssd_conv_w[j], ssd_conv_b[j],
                                 ssd_dt_bias[j], ssd_a_log[j], ssd_d[j], ssd_norm_g[j], ssd_w_out[j],
                                 state_ssd_fwd[:, j], state_ssd_bwd[:, j])
            sf_new.append(s_f)
            sb_new.append(s_b)
        else:
            x, ckv, kpe = _mla(tok, x, mods, base, norm1_g[i], mla_w_dq[j], mla_q_norm_g[j], mla_w_uq[j],
                               mla_w_dkv[j], mla_kv_norm_g[j], mla_w_ukv[j], mla_w_o[j],
                               cache_mla_ckv[:, j], cache_mla_kpe[:, j])
            ckv_new.append(ckv)
            kpe_new.append(kpe)
        k = i // 2
        if i % 2 == 0:
            x = _dense_ffn(tok, x, mods, base, norm2_g[i], k, ffn_w_gate, ffn_w_up, ffn_w_down)
        else:
            x = _moe_ffn(tok, x, mods, base, norm2_g[i], k, moe_w_router[k], moe_b_router[k], moe_w_gate,
                         moe_w_up, moe_w_down)
    y = _rmsnorm(x, norm_f_g, tok.row_tile(512))
    y_prompt = y[:tok.tp].reshape(bp, lp, d)
    y_sample = y[tok.tp:].reshape(bs, ls, d)
    return (y_prompt, y_sample, jnp.stack(ckv_new, axis=1), jnp.stack(kpe_new, axis=1),
            jnp.stack(sf_new, axis=1), jnp.stack(sb_new, axis=1))
```

```python
import functools
import math

import jax
import jax.numpy as jnp
from jax import lax
from jax.experimental import pallas as pl
from jax.experimental.pallas import tpu as pltpu

F32 = jnp.float32
BF16 = jnp.bfloat16
EPS = 1e-6
N_MIXERS = 3
MLA_HEADS = 16
GRID_W = 64
ROPE_BASE = 10000.0
HY_BANDS = 16
HY_SIN_FREQ = 1.0
SSD_CHUNK = 128
LANES = 128
VMEM_LIMIT_BYTES = 56 * 1024 * 1024
HIGHEST = lax.Precision.HIGHEST


def _cparams(n_axes):
    return pltpu.CompilerParams(dimension_semantics=("arbitrary",) * n_axes,
                                vmem_limit_bytes=VMEM_LIMIT_BYTES)


def _pick(n, candidates):
    for c in candidates:
        if n % c == 0:
            return c
    raise ValueError(f"no tile for {n} in {candidates}")


def _silu(x):
    return x * (1.0 / (1.0 + jnp.exp(-x)))


def _dot(a, b):
    return jnp.dot(a, b, preferred_element_type=F32)


def _dot_nt(a, b):
    return lax.dot_general(a, b, (((1,), (1,)), ((), ())), preferred_element_type=F32)


def _dot_tn(a, b):
    return lax.dot_general(a, b, (((0,), (0,)), ((), ())), preferred_element_type=F32)


class _Tokens:
    def __init__(self, bp, lp, bs, ls):
        self.bp, self.lp, self.bs, self.ls = bp, lp, bs, ls
        self.tp, self.ts = bp * lp, bs * ls
        self.t = self.tp + self.ts
        self.groups = 1 + bs
        assert self.tp % ls == 0 or self.tp % lp == 0

    def row_tile(self, cap=1024):
        return _pick(math.gcd(self.tp, self.ls), [c for c in (1024, 512, 256, 128, 64, 32, 16, 8) if c <= cap])

    def group_of(self, row_start):
        return jnp.where(row_start < self.tp, 0, 1 + (row_start - self.tp) // self.ls)


def _mm_body(*refs, pre, swiglu, epi, nk, nj, grouped, slab_in, slab_out, precise, tn):
    refs = list(refs)
    tile_row_ref = None
    if grouped:
        refs.pop(0)
        tile_row_ref = refs.pop(0)
    x_ref = refs.pop(0)
    g_ref = refs.pop(0) if pre else None
    sh_ref = sc_ref = None
    if pre == "normmod":
        sh_ref = refs.pop(0)
        sc_ref = refs.pop(0)
    w_ref = refs.pop(0)
    w2_ref = refs.pop(0) if swiglu else None
    res_ref = gate_ref = None
    if epi == "resgate":
        res_ref = refs.pop(0)
        gate_ref = refs.pop(0)
    o_ref = refs.pop(0)
    h_scr = refs.pop(0) if (pre or slab_in) else None
    acc_ref = refs.pop(0) if nk > 1 else None
    i = pl.program_id(0)
    j = pl.program_id(1)
    k = pl.program_id(2)

    def prologue():
        if slab_in:
            tm, kdim = h_scr.shape
            slab = kdim // LANES
            for s in range(slab):
                h_scr[:, s * LANES:(s + 1) * LANES] = x_ref[pl.ds(s, tm, stride=slab), :].astype(BF16)
        else:
            xf = x_ref[...].astype(F32)
            ms = jnp.mean(xf * xf, axis=-1, keepdims=True)
            y = (xf * lax.rsqrt(ms + EPS)) * g_ref[...]
            if pre == "normmod":
                y = y * (1.0 + sc_ref[0]) + sh_ref[0]
            h_scr[...] = y.astype(BF16)

    def product(wr):
        if precise:
            return jnp.dot(x_ref[...], wr[...], preferred_element_type=F32, precision=HIGHEST)
        if pre or slab_in:
            xb = h_scr[...]
        elif x_ref.shape[1] != wr.shape[0]:
            tk = wr.shape[0]
            xb = x_ref[:, pl.ds(pl.multiple_of(k * tk, tk), tk)].astype(BF16)
        else:
            xb = x_ref[...].astype(BF16)
        return _dot(xb, wr[...].astype(BF16))

    def store_slab(a, s0):
        slab = (nj * tn) // LANES
        tm = o_ref.shape[0] // slab
        for q in range(tn // LANES):
            o_ref[pl.ds(s0 + q, tm, stride=slab), :] = a[:, q * LANES:(q + 1) * LANES].astype(o_ref.dtype)

    def finish(a, a2):
        if swiglu:
            a = _silu(a) * a2
        if epi == "resgate":
            a = res_ref[...] + gate_ref[0] * a
        if slab_out:
            for jj in range(nj):
                pl.when(j == jj)(functools.partial(store_slab, a, jj * (tn // LANES)))
        else:
            o_ref[...] = a.astype(o_ref.dtype)

    def compute():
        if pre or slab_in:
            pl.when(j == 0)(prologue)
        a = product(w_ref)
        a2 = product(w2_ref) if swiglu else None
        if nk == 1:
            finish(a, a2)
        else:
            @pl.when(k == 0)
            def _():
                acc_ref[...] = a

            @pl.when((k > 0) & (k < nk - 1))
            def _():
                acc_ref[...] += a

            @pl.when(k == nk - 1)
            def _():
                finish(acc_ref[...] + a, None)

    if grouped:
        used = tile_row_ref[i] == i
        pl.when(used)(compute)

        @pl.when(jnp.logical_not(used))
        def _():
            o_ref[...] = jnp.zeros_like(o_ref)
    else:
        compute()


def _matmul(x, w, *, tm, tn, tk=None, w2=None, pre=None, g=None, mods=None, mod_base=None, tok=None,
            shift_idx=None, scale_idx=None, epi=None, res=None, gate_idx=None, out_dtype=F32,
            x_col0=0, n_rows=None, row0=0, w_col0=0, n_cols=None, w_lead=(), tiles=None, slab_in=False,
            slab_out=False, x_resident=False, precise=False, name="mm"):
    grouped = tiles is not None
    kdim = w.shape[-2]
    n = (w.shape[-1] - w_col0) if n_cols is None else n_cols
    tk = kdim if tk is None else tk
    nk = kdim // tk
    assert kdim % tk == 0 and n % tn == 0 and w_col0 % tn == 0
    wcb0 = w_col0 // tn
    swiglu = w2 is not None
    assert nk == 1 or not (pre or swiglu)
    x_resident = x_resident and nk > 1
    if slab_in:
        assert x.shape[1] == LANES and kdim % LANES == 0 and nk == 1 and not pre
        rows = x.shape[0] // (kdim // LANES) if n_rows is None else n_rows
    else:
        rows = (x.shape[0] - row0) if n_rows is None else n_rows
    assert rows % tm == 0 and row0 % tm == 0
    ni, nj = (tiles[0].shape[0] if grouped else rows // tm), n // tn
    rb0 = row0 // tm
    xcb0 = x_col0 // tk
    assert x_col0 % tk == 0

    def rowblk(i, pref):
        return pref[1][i] if grouped else i + rb0

    def colblk(i, j, pref):
        return jnp.where(pref[1][i] == i, j, nj - 1) if grouped else j

    def kblk(i, k, pref):
        return jnp.where(pref[1][i] == i, k, nk - 1) if grouped else k

    def grp(i):
        return tok.group_of(i * tm)

    in_specs, args = [], []
    once = dict(pipeline_mode=pl.Buffered(1)) if (pre or slab_in or x_resident) else {}
    if slab_in:
        in_specs.append(pl.BlockSpec((tm * (kdim // LANES), LANES), lambda i, j, k, *p: (rowblk(i, p), 0), **once))
    elif x_resident:
        assert x_col0 == 0 and x.shape[1] == kdim and nk > 1
        in_specs.append(pl.BlockSpec((tm, kdim), lambda i, j, k, *p: (rowblk(i, p), 0), **once))
    else:
        in_specs.append(pl.BlockSpec((tm, tk), lambda i, j, k, *p: (rowblk(i, p), xcb0 + kblk(i, k, p)), **once))
    args.append(x)
    if pre:
        in_specs.append(pl.BlockSpec((1, tk), lambda i, j, k, *p: (0, 0)))
        args.append(g.reshape(1, tk).astype(F32))
    if pre == "normmod":
        for idx in (shift_idx, scale_idx):
            in_specs.append(pl.BlockSpec((1, 1, tk), lambda i, j, k, *p, idx=idx: (mod_base + grp(i) * 6 + idx, 0, 0)))
            args.append(mods)
    lead = tuple(w_lead)
    none = (None,) * len(lead)
    wspec = (pl.BlockSpec(none + (None, tk, tn),
                          lambda i, j, k, *p: lead + (p[0][i], kblk(i, k, p), wcb0 + colblk(i, j, p))) if grouped
             else pl.BlockSpec(none + (tk, tn), lambda i, j, k, *p: lead + (k, wcb0 + j)))
    in_specs.append(wspec)
    args.append(w)
    if swiglu:
        in_specs.append(wspec)
        args.append(w2)
    if epi == "resgate":
        in_specs.append(pl.BlockSpec((tm, tn), lambda i, j, k, *p: (i + rb0, j)))
        args.append(res)
        in_specs.append(pl.BlockSpec((1, 1, tn), lambda i, j, k, *p: (mod_base + grp(i) * 6 + gate_idx, 0, j)))
        args.append(mods)
    if slab_out:
        assert tn % LANES == 0
        out_shape = jax.ShapeDtypeStruct((rows * (n // LANES), LANES), out_dtype)
        out_spec = pl.BlockSpec((tm * (n // LANES), LANES), lambda i, j, k, *p: (i, 0))
    else:
        out_shape = jax.ShapeDtypeStruct((rows, n), out_dtype)
        out_spec = pl.BlockSpec((tm, tn), lambda i, j, k, *p: (i, j))
    scratch = []
    if pre or slab_in:
        scratch.append(pltpu.VMEM((tm, tk), BF16))
    if nk > 1:
        scratch.append(pltpu.VMEM((tm, tn), F32))
    body = functools.partial(_mm_body, pre=pre, swiglu=swiglu, epi=epi, nk=nk, nj=nj, grouped=grouped,
                             slab_in=slab_in, slab_out=slab_out, precise=precise, tn=tn)
    grid_spec = pltpu.PrefetchScalarGridSpec(
        num_scalar_prefetch=2 if grouped else 0, grid=(ni, nj, nk),
        in_specs=in_specs, out_specs=out_spec, scratch_shapes=scratch)
    call = pl.pallas_call(body, grid_spec=grid_spec, out_shape=out_shape,
                          compiler_params=_cparams(3), name=name)
    return call(*tiles, *args) if grouped else call(*args)


def _mod_body(c_ref, w_ref, b_ref, o_ref):
    cb = _silu(c_ref[...]).astype(BF16)
    o_ref[...] = _dot(cb, w_ref[...].astype(BF16)) + b_ref[...]


def _modulation(cond, w_mod, b_mod):
    depth, d, n = w_mod.shape
    gp = cond.shape[0]
    tn = _pick(n, (1024, 512, 256, 128))
    return pl.pallas_call(
        _mod_body,
        grid=(depth, n // tn),
        in_specs=[pl.BlockSpec((gp, d), lambda i, j: (0, 0)),
                  pl.BlockSpec((None, d, tn), lambda i, j: (i, 0, j)),
                  pl.BlockSpec((None, 1, tn), lambda i, j: (i, 0, j))],
        out_specs=pl.BlockSpec((None, gp, tn), lambda i, j: (i, 0, j)),
        out_shape=jax.ShapeDtypeStruct((depth, gp, n), F32),
        compiler_params=_cparams(2), name="modulation",
    )(cond, w_mod, b_mod.reshape(depth, 1, n))


def _rms_body(x_ref, g_ref, o_ref):
    xf = x_ref[...]
    ms = jnp.mean(xf * xf, axis=-1, keepdims=True)
    o_ref[...] = (xf * lax.rsqrt(ms + EPS)) * g_ref[...]


def _rmsnorm(x, g, tm):
    t, d = x.shape
    return pl.pallas_call(
        _rms_body, grid=(t // tm,),
        in_specs=[pl.BlockSpec((tm, d), lambda i: (i, 0)), pl.BlockSpec((1, d), lambda i: (0, 0))],
        out_specs=pl.BlockSpec((tm, d), lambda i: (i, 0)),
        out_shape=jax.ShapeDtypeStruct((t, d), F32),
        compiler_params=_cparams(1), name="final_norm",
    )(x, g.reshape(1, d))


def _dft_tables(l):
    f = jnp.arange(l, dtype=jnp.int32)
    m = (f[:, None] * f[None, :]) % (2 * l)
    ang = m.astype(F32) * (math.pi / l)
    return jnp.cos(ang), jnp.sin(ang)


def _hy_features(l):
    t = jnp.arange(l, dtype=F32)
    t01 = t / (l - 1)
    bands = jnp.linspace(1e-4, HY_BANDS - 1, HY_BANDS, dtype=F32)
    ang = (2.0 * math.pi / l) * t[:, None] * bands[None, :]
    z = jnp.concatenate([t01[:, None], jnp.cos(ang), jnp.sin(ang)], axis=-1)
    return jnp.pad(z, ((0, 0), (0, 128 - z.shape[1])))


def _hy_taps_body(z_ref, w1_ref, b1_ref, w2_ref, b2_ref, w3f_ref, w3b_ref, dec_ref, ksum_ref, kdif_ref, knyq_ref):
    l = z_ref.shape[0]
    dot = functools.partial(jnp.dot, preferred_element_type=F32, precision=HIGHEST)
    f = jnp.sin(HY_SIN_FREQ * (dot(z_ref[...], w1_ref[...]) + b1_ref[...]))
    f = jnp.sin(HY_SIN_FREQ * (dot(f, w2_ref[...]) + b2_ref[...]))
    row = lax.broadcasted_iota(jnp.int32, (l, 1), 0)
    t01 = row.astype(F32) / (l - 1)
    kf = dot(f, w3f_ref[...]) * jnp.exp(-t01 * dec_ref[0:1, :])
    kb = dot(f, w3b_ref[...]) * jnp.exp(-t01 * dec_ref[1:2, :])
    kb = jnp.where(row == 0, 0.0, kb)
    l1 = jnp.sum(jnp.abs(kf), axis=0, keepdims=True) + jnp.sum(jnp.abs(kb), axis=0, keepdims=True)
    kf = kf / l1
    kb = kb / l1
    ks = kf + kb
    alt = jnp.where(row % 2 == 0, 1.0, -1.0)
    ksum_ref[...] = ks
    kdif_ref[...] = kb - kf
    knyq_ref[...] = jnp.sum(ks * alt, axis=0, keepdims=True)


def _hy_taps(l, w1, b1, w2, b2, w3, decay):
    d = decay.shape[1]
    nf = w2.shape[0]
    cb = _pick(d, (512, 256, 128))
    z = _hy_features(l)
    w1p = jnp.pad(w1, ((0, 128 - w1.shape[0]), (0, 0)))
    full = lambda shape: pl.BlockSpec(shape, lambda c: (0,) * len(shape))
    return pl.pallas_call(
        _hy_taps_body, grid=(d // cb,),
        in_specs=[full((l, 128)), full((128, nf)), full((1, nf)), full((nf, nf)), full((1, nf)),
                  pl.BlockSpec((nf, cb), lambda c: (0, c)),
                  pl.BlockSpec((nf, cb), lambda c: (0, d // cb + c)),
                  pl.BlockSpec((2, cb), lambda c: (0, c))],
        out_specs=[pl.BlockSpec((l, cb), lambda c: (0, c)), pl.BlockSpec((l, cb), lambda c: (0, c)),
                   pl.BlockSpec((1, cb), lambda c: (0, c))],
        out_shape=[jax.ShapeDtypeStruct((l, d), F32), jax.ShapeDtypeStruct((l, d), F32),
                   jax.ShapeDtypeStruct((1, d), F32)],
        compiler_params=_cparams(1), name="hyena_taps",
    )(z, w1p, b1.reshape(1, nf), w2, b2.reshape(1, nf), w3, w3, decay)


def _dwconv3(a, w_ref, b_ref, row, l):
    prev = jnp.where(row == 0, 0.0, pltpu.roll(a, 1, axis=0))
    nxt = jnp.where(row == l - 1, 0.0, pltpu.roll(a, l - 1, axis=0))
    return prev * w_ref[0:1, :] + a * w_ref[1:2, :] + nxt * w_ref[2:3, :] + b_ref[...]


def _hy_core_body(x0_ref, x1_ref, v_ref, w0_ref, w1_ref, wv_ref, b0_ref, b1_ref, bv_ref, skip_ref,
                  kr_ref, ki_ref, knyq_ref, c_ref, s_ref, o_ref):
    l = x0_ref.shape[0]
    row = lax.broadcasted_iota(jnp.int32, (l, 1), 0)
    x0 = _dwconv3(x0_ref[...], w0_ref, b0_ref, row, l)
    x1 = _dwconv3(x1_ref[...], w1_ref, b1_ref, row, l)
    v = _dwconv3(v_ref[...], wv_ref, bv_ref, row, l)
    z = v * x1
    zb = z.astype(BF16)
    cm = c_ref[...]
    sm = s_ref[...]
    zr = _dot(cm, zb)
    zs = _dot(sm, zb)
    wf = jnp.where(row == 0, 0.5 / l, 1.0 / l)
    kr = kr_ref[...] * wf
    ki = ki_ref[...] * wf
    a = (zr * kr + zs * ki).astype(BF16)
    b = (zs * kr - zr * ki).astype(BF16)
    alt = jnp.where(row % 2 == 0, 1.0, -1.0)
    znyq = jnp.sum(z * alt, axis=0, keepdims=True)
    y = _dot(cm, a) + _dot(sm, b) + alt * (znyq * knyq_ref[...] * (0.5 / l))
    o_ref[...] = (x0 * (y + z * skip_ref[...])).astype(o_ref.dtype)


def _hy_core(u, row0, nseq, l, d, conv_w, conv_b, skip, kr, ki, knyq, cmat, smat):
    cb = _pick(d, (128,) if l > 512 else (512, 256, 128))
    nd = d // cb
    sb0 = row0 // l
    assert row0 % l == 0
    useq = lambda sec: pl.BlockSpec((l, cb), lambda s, c, sec=sec: (sb0 + s, sec * nd + c))
    wsec = lambda sec: pl.BlockSpec((3, cb), lambda s, c, sec=sec: (0, sec * nd + c))
    bsec = lambda sec: pl.BlockSpec((1, cb), lambda s, c, sec=sec: (0, sec * nd + c))
    col = lambda rows: pl.BlockSpec((rows, cb), lambda s, c: (0, c))
    const = pl.BlockSpec((l, l), lambda s, c: (0, 0), pipeline_mode=pl.Buffered(1))
    cb3 = conv_b.reshape(1, 3 * d)
    return pl.pallas_call(
        _hy_core_body, grid=(nseq, nd),
        in_specs=[useq(0), useq(1), useq(2), wsec(0), wsec(1), wsec(2), bsec(0), bsec(1), bsec(2),
                  col(1), col(l), col(l), col(1), const, const],
        out_specs=pl.BlockSpec((l, cb), lambda s, c: (s, c)),
        out_shape=jax.ShapeDtypeStruct((nseq * l, d), BF16),
        compiler_params=_cparams(2), name=f"hyena_core_{l}",
    )(u, u, u, conv_w, conv_w, conv_w, cb3, cb3, cb3, skip.reshape(1, d), kr, ki, knyq, cmat, smat)


def _hyena(tok, x, mods, mod_base, g, lyr, w_in, conv_w, conv_b, f_w1, f_b1, f_w2, f_b2, f_w3, decay, skip, w_out):
    d = x.shape[1]
    tm = tok.row_tile()
    u = _matmul(x, w_in, w_lead=(lyr,), tm=tm, tn=_pick(3 * d, (512, 256, 128)), pre="normmod", g=g, mods=mods,
                mod_base=mod_base, tok=tok, shift_idx=0, scale_idx=1, name="hyena_in")
    parts = []
    for row0, nseq, l in ((0, tok.bp, tok.lp), (tok.tp, tok.bs, tok.ls)):
        ksum, kdif, knyq = _hy_taps(l, f_w1, f_b1, f_w2, f_b2, f_w3, decay)
        cmat, smat = _dft_tables(l)
        tf = _pick(l, (512, 256, 128))
        tn = _pick(d, (512, 256, 128))
        kr = _matmul(cmat, ksum, tm=tf, tn=tn, precise=True, name="hyena_spec_re")
        ki = _matmul(smat, kdif, tm=tf, tn=tn, precise=True, name="hyena_spec_im")
        parts.append(_hy_core(u, row0, nseq, l, d, conv_w, conv_b, skip, kr, ki, knyq,
                              cmat.astype(BF16), smat.astype(BF16)))
    gmix = jnp.concatenate(parts, axis=0)
    return _matmul(gmix, w_out, w_lead=(lyr,), tm=tm, tn=_pick(d, (512, 256, 128)), epi="resgate", res=x,
                   mods=mods, mod_base=mod_base, tok=tok, gate_idx=2, name="hyena_out")


def _ssd_conv_body(x_ref, w_ref, b_ref, o_ref):
    l = x_ref.shape[0]
    row = lax.broadcasted_iota(jnp.int32, (l, 1), 0)
    o_ref[...] = _silu(_dwconv3(x_ref[...], w_ref, b_ref, row, l))


def _ssd_conv(proj, col0, conv_w, conv_b, row0, nseq, l):
    cdim = conv_w.shape[1]
    cb = _pick(math.gcd(cdim, col0), (512, 256, 128))
    cb0, sb0 = col0 // cb, row0 // l
    return pl.pallas_call(
        _ssd_conv_body, grid=(nseq, cdim // cb),
        in_specs=[pl.BlockSpec((l, cb), lambda s, c: (sb0 + s, cb0 + c)),
                  pl.BlockSpec((3, cb), lambda s, c: (0, c)),
                  pl.BlockSpec((1, cb), lambda s, c: (0, c))],
        out_specs=pl.BlockSpec((l, cb), lambda s, c: (s, c)),
        out_shape=jax.ShapeDtypeStruct((nseq * l, cdim), F32),
        compiler_params=_cparams(2), name=f"ssd_conv_{l}",
    )(proj, conv_w, conv_b.reshape(1, cdim))


def _ssd_dt_body(raw_ref, bias_ref, alog_ref, dt_ref, cum_ref):
    tm, w = raw_ref.shape
    xr = raw_ref[...] + bias_ref[...]
    dt = jnp.maximum(xr, 0.0) + jnp.log1p(jnp.exp(-jnp.abs(xr)))
    dt_ref[...] = dt
    a = dt * (-jnp.exp(alog_ref[...]))
    ri = lax.broadcasted_iota(jnp.int32, (SSD_CHUNK, SSD_CHUNK), 0)
    ci = lax.broadcasted_iota(jnp.int32, (SSD_CHUNK, SSD_CHUNK), 1)
    lower = (ci <= ri).astype(F32)
    upper = (ci >= ri).astype(F32)
    lane = lax.broadcasted_iota(jnp.int32, (SSD_CHUNK, w), 1)
    for q in range(tm // SSD_CHUNK):
        ac = a[q * SSD_CHUNK:(q + 1) * SSD_CHUNK, :]
        pc = jnp.dot(lower, ac, preferred_element_type=F32, precision=HIGHEST)
        rc = jnp.dot(upper, ac, preferred_element_type=F32, precision=HIGHEST)
        cum_ref[q * SSD_CHUNK:(q + 1) * SSD_CHUNK, :] = jnp.where(lane < w // 2, pc, rc)


def _ssd_dt(proj, col0, dt_bias, a_log, tm):
    t = proj.shape[0]
    w = dt_bias.size
    assert col0 % w == 0
    return pl.pallas_call(
        _ssd_dt_body, grid=(t // tm,),
        in_specs=[pl.BlockSpec((tm, w), lambda i: (i, col0 // w)),
                  pl.BlockSpec((1, w), lambda i: (0, 0)), pl.BlockSpec((1, w), lambda i: (0, 0))],
        out_specs=[pl.BlockSpec((tm, w), lambda i: (i, 0)), pl.BlockSpec((tm, w), lambda i: (i, 0))],
        out_shape=[jax.ShapeDtypeStruct((t, w), F32), jax.ShapeDtypeStruct((t, w), F32)],
        compiler_params=_cparams(1), name="ssd_dt",
    )(proj, dt_bias.reshape(1, w), a_log.reshape(1, w))


def _ssd_scan_body(*refs, has_init, r, p, nc):
    refs = list(refs)
    x_ref, b_ref, c_ref, dsk_ref = refs[:4]
    dt_refs, cum_refs, cumt_refs = refs[4:6], refs[6:8], refs[8:10]
    refs = refs[10:]
    init_refs = (refs.pop(0), refs.pop(0)) if has_init else None
    y_ref, fin_refs, st_ref = refs[0], refs[1:3], refs[3]
    ch = SSD_CHUNK
    npair = r // 2
    lane = lax.broadcasted_iota(jnp.int32, (ch, 2 * p), 1)
    left = lane < p
    ri = lax.broadcasted_iota(jnp.int32, (ch, ch), 0)
    ci = lax.broadcasted_iota(jnp.int32, (ch, ch), 1)

    for d in range(2):
        keep = (ci <= ri) if d == 0 else (ci >= ri)
        for pr in range(npair):
            if has_init:
                st_ref[pr] = init_refs[d][pr * 2 * p:(pr + 1) * 2 * p, :].T
            else:
                st_ref[pr] = jnp.zeros((st_ref.shape[1], 2 * p), F32)

        def chunk(ic, carry, d=d, keep=keep):
            cidx = ic if d == 0 else nc - 1 - ic
            r0 = pl.multiple_of(cidx * ch, ch)
            rows = pl.ds(r0, ch)
            bc = b_ref[rows, :].astype(BF16)
            cc = c_ref[rows, :].astype(BF16)
            cbm = _dot_nt(cc, bc)
            bt = b_ref[rows, :].T.astype(BF16)
            dtc = dt_refs[d][rows, :]
            cumc = cum_refs[d][rows, :]
            cumt = cumt_refs[d][:, rows]
            edge = cumc[0:1, :] if d == 1 else cumc[ch - 1:ch, :]
            for pr in range(npair):
                xpair = x_ref[rows, pr * 2 * p:(pr + 1) * 2 * p]
                ha, hb = 2 * pr, 2 * pr + 1
                sel = lambda col: jnp.where(left, col[:, ha:ha + 1], col[:, hb:hb + 1])
                xs = xpair * sel(dtc)
                xsb = xs.astype(BF16)
                ydiag = []
                for h in (ha, hb):
                    seg = cumc[:, h:h + 1] - cumt[h:h + 1, :]
                    dec = jnp.exp(jnp.where(keep, seg, -jnp.inf))
                    ydiag.append(_dot((cbm * dec).astype(BF16), xsb))
                cum_pair = sel(cumc)
                st = st_ref[pr]
                yoff = _dot(cc, st.astype(BF16)) * jnp.exp(cum_pair)
                yc = jnp.where(left, ydiag[0], ydiag[1]) + yoff
                edge_pair = jnp.where(left[0:1, :], edge[:, ha:ha + 1], edge[:, hb:hb + 1])
                xsd = (xs * jnp.exp(edge_pair - cum_pair)).astype(BF16)
                st_ref[pr] = st * jnp.exp(edge_pair) + _dot(bt, xsd)
                cols = slice(pr * 2 * p, (pr + 1) * 2 * p)
                if d == 0:
                    y_ref[rows, cols] = yc + xpair * dsk_ref[:, cols]
                else:
                    y_ref[rows, cols] += yc
            return carry

        lax.fori_loop(0, nc, chunk, 0)
        for pr in range(npair):
            fin_refs[d][pr * 2 * p:(pr + 1) * 2 * p, :] = st_ref[pr].T


def _ssd_scan(xbc, dtg, cumg, cumtg, d_vec, nseq, l, di, g, n, p, init):
    h = di // p
    r = h // g
    rp = r * p
    assert r % 2 == 0 and rp % 128 == 0 and n % 128 == 0 and l % SSD_CHUNK == 0
    nb0, nc0 = di // n, (di + g * n) // n
    has_init = init is not None
    dspec = lambda d: pl.BlockSpec((None, l, r), lambda s, gi, d=d: (d * g + gi, s, 0))
    tspec = lambda d: pl.BlockSpec((None, r, l), lambda s, gi, d=d: (d * g + gi, 0, s))
    sspec = pl.BlockSpec((None, rp, n), lambda s, gi: (s, gi, 0))
    in_specs = [pl.BlockSpec((l, rp), lambda s, gi: (s, gi)),
                pl.BlockSpec((l, n), lambda s, gi: (s, nb0 + gi)),
                pl.BlockSpec((l, n), lambda s, gi: (s, nc0 + gi)),
                pl.BlockSpec((1, rp), lambda s, gi: (0, gi)),
                dspec(0), dspec(1), dspec(0), dspec(1), tspec(0), tspec(1)]
    args = [xbc, xbc, xbc, d_vec, dtg, dtg, cumg, cumg, cumtg, cumtg]
    if has_init:
        in_specs += [sspec, sspec]
        args += [s.reshape(nseq, h * p, n) for s in init]
    st_shape = jax.ShapeDtypeStruct((nseq, h * p, n), F32)
    body = functools.partial(_ssd_scan_body, has_init=has_init, r=r, p=p, nc=l // SSD_CHUNK)
    y, sf, sb = pl.pallas_call(
        body, grid=(nseq, g), in_specs=in_specs,
        out_specs=[pl.BlockSpec((l, rp), lambda s, gi: (s, gi)), sspec, sspec],
        out_shape=[jax.ShapeDtypeStruct((nseq * l, di), F32), st_shape, st_shape],
        scratch_shapes=[pltpu.VMEM((r // 2, n, 2 * p), F32)],
        compiler_params=_cparams(2), name=f"ssd_scan_{l}",
    )(*args)
    return y, sf.reshape(nseq, h, p, n), sb.reshape(nseq, h, p, n)


def _ssd_gate_body(y_ref, z_ref, g_ref, o_ref):
    v = y_ref[...] * _silu(z_ref[...])
    ms = jnp.mean(v * v, axis=-1, keepdims=True)
    o_ref[...] = ((v * lax.rsqrt(ms + EPS)) * g_ref[...]).astype(o_ref.dtype)


def _ssd_gate(y, proj, norm_g, tm):
    t, di = y.shape
    return pl.pallas_call(
        _ssd_gate_body, grid=(t // tm,),
        in_specs=[pl.BlockSpec((tm, di), lambda i: (i, 0)), pl.BlockSpec((tm, di), lambda i: (i, 0)),
                  pl.BlockSpec((1, di), lambda i: (0, 0))],
        out_specs=pl.BlockSpec((tm, di), lambda i: (i, 0)),
        out_shape=jax.ShapeDtypeStruct((t, di), BF16),
        compiler_params=_cparams(1), name="ssd_gate",
    )(y, proj, norm_g.reshape(1, di))


def _ssd(tok, x, mods, mod_base, gnorm, w_in, conv_w, conv_b, dt_bias, a_log, d_skip, norm_g, w_out,
         init_f, init_b):
    d = x.shape[1]
    p, n = init_f.shape[-2], init_f.shape[-1]
    h = dt_bias.shape[1]
    di = h * p
    cdim = conv_w.shape[1]
    g = (cdim - di) // (2 * n)
    r = h // g
    tm = tok.row_tile()
    pre_kw = dict(pre="normmod", g=gnorm, mods=mods, mod_base=mod_base, tok=tok, shift_idx=0, scale_idx=1)
    proj = _matmul(x, w_in, tm=tm, tn=_pick(math.gcd(di, cdim), (512, 256, 128)), n_cols=di + cdim,
                   name="ssd_in", **pre_kw)
    dt_raw = _matmul(x, w_in, tm=tm, tn=2 * h, w_col0=di + cdim, name="ssd_in_dt", **pre_kw)
    dt, cum = _ssd_dt(dt_raw, 0, dt_bias, a_log, _pick(tok.t, (512, 256, 128)))
    split = lambda a: jnp.transpose(a.reshape(tok.t, 2 * g, r), (1, 0, 2))
    dtg, cumg = split(dt), split(cum)
    cumtg = jnp.transpose(cumg, (0, 2, 1))
    d_vec = jnp.repeat(d_skip.astype(F32), p).reshape(1, di)
    ys, states = [], []
    for row0, nseq, l, init in ((0, tok.bp, tok.lp, None), (tok.tp, tok.bs, tok.ls, (init_f, init_b))):
        xbc = _ssd_conv(proj, di, conv_w, conv_b, row0, nseq, l)
        sl = slice(row0, row0 + nseq * l)
        y, sf, sb = _ssd_scan(xbc, dtg[:, sl], cumg[:, sl], cumtg[:, :, sl], d_vec, nseq, l, di, g, n, p, init)
        ys.append(y)
        states.append((sf, sb))
    yg = _ssd_gate(jnp.concatenate(ys, axis=0), proj, norm_g, _pick(tok.t, (256, 128)))
    out = _matmul(yg, w_out, tm=tm, tn=_pick(d, (512, 256, 128)), tk=_pick(di, (2048, 1024, 512, 256)),
                  epi="resgate", res=x, mods=mods, mod_base=mod_base, tok=tok, gate_idx=2, name="ssd_out")
    return out, states[0]


def _rope_tables(n_tokens, dr):
    axis_dim = dr // 2
    rows = n_tokens // GRID_W
    row = jnp.broadcast_to(jnp.arange(rows, dtype=F32)[:, None], (rows, GRID_W)).reshape(-1)
    col = jnp.broadcast_to(jnp.arange(GRID_W, dtype=F32)[None, :], (rows, GRID_W)).reshape(-1)
    inv_freq = ROPE_BASE ** (-jnp.arange(0, axis_dim, 2, dtype=F32) / axis_dim)
    ang = jnp.concatenate([row[:, None] * inv_freq, col[:, None] * inv_freq], axis=-1)
    return jnp.cos(ang), jnp.sin(ang)


def _rot_cols(w, dr):
    half = dr // 2
    return jnp.concatenate([-w[..., half:], w[..., :half]], axis=-1)


def _mla_kv_body(kv_ref, pe_ref, g_ref, cs_ref, ckv_ref, kpe_ref):
    kv = kv_ref[...]
    ms = jnp.mean(kv * kv, axis=-1, keepdims=True)
    ckv_ref[...] = (kv * lax.rsqrt(ms + EPS)) * g_ref[...]
    prod = pe_ref[...] * cs_ref[...]
    kpe_ref[...] = prod + pltpu.roll(prod, prod.shape[1] // 2, axis=1)


def _mla_kv(qkv, col0, rank, dr, g, cs, tm):
    t = qkv.shape[0]
    assert col0 % rank == 0 and (col0 + rank) % (2 * dr) == 0 and 2 * dr == 128
    return pl.pallas_call(
        _mla_kv_body, grid=(t // tm,),
        in_specs=[pl.BlockSpec((tm, rank), lambda i: (i, col0 // rank)),
                  pl.BlockSpec((tm, 2 * dr), lambda i: (i, (col0 + rank) // (2 * dr))),
                  pl.BlockSpec((1, rank), lambda i: (0, 0)),
                  pl.BlockSpec((tm, 2 * dr), lambda i: (i, 0))],
        out_specs=[pl.BlockSpec((tm, rank), lambda i: (i, 0)), pl.BlockSpec((tm, 2 * dr), lambda i: (i, 0))],
        out_shape=[jax.ShapeDtypeStruct((t, rank), F32), jax.ShapeDtypeStruct((t, 2 * dr), F32)],
        compiler_params=_cparams(1), name="mla_kv",
    )(qkv, qkv, g.reshape(1, rank), cs)


def _attn_body(qn_ref, qp_ref, qr_ref, cos_ref, sin_ref, kv_ref, kpe_ref, o_ref, *, dn, dr, dv, scale):
    qrot = qp_ref[...] * cos_ref[...] + qr_ref[...] * sin_ref[...]
    lane = lax.broadcasted_iota(jnp.int32, qrot.shape, 1)
    kpe = kpe_ref[...].astype(BF16)
    for h in range(2):
        qn = qn_ref[:, h * dn:(h + 1) * dn].astype(BF16)
        kn = kv_ref[:, h * (dn + dv):h * (dn + dv) + dn]
        v = kv_ref[:, h * (dn + dv) + dn:(h + 1) * (dn + dv)]
        qh = jnp.where((lane >= h * dr) & (lane < (h + 1) * dr), qrot, 0.0).astype(BF16)
        s = _dot_nt(jnp.concatenate([qn, qh], axis=1), jnp.concatenate([kn, kpe], axis=1))
        m = jnp.max(s, axis=-1, keepdims=True)
        e = jnp.exp((s - m) * scale)
        o = _dot(e.astype(BF16), v) / jnp.sum(e, axis=-1, keepdims=True)
        o_ref[:, h * dv:(h + 1) * dv] = o.astype(o_ref.dtype)


def _attention(q, cosq, sinq, kv, kpe, row0, nb, lq, lk, heads, dn, dr, dv):
    assert dn == 128 and dv == 128 and dr == 64 and heads % 2 == 0
    tq = _pick(lq, (512, 256, 128))
    nq = lq // tq
    rb0 = row0 // tq
    hp = heads // 2
    pe0, rot0 = heads * dn // 128, (heads * dn + heads * dr) // 128
    body = functools.partial(_attn_body, dn=dn, dr=dr, dv=dv, scale=(dn + dr) ** -0.5)
    return pl.pallas_call(
        body, grid=(nb, hp, nq),
        in_specs=[pl.BlockSpec((tq, 2 * dn), lambda b, h, i: (rb0 + b * nq + i, h)),
                  pl.BlockSpec((tq, 128), lambda b, h, i: (rb0 + b * nq + i, pe0 + h)),
                  pl.BlockSpec((tq, 128), lambda b, h, i: (rb0 + b * nq + i, rot0 + h)),
                  pl.BlockSpec((tq, 128), lambda b, h, i: (rb0 + b * nq + i, 0)),
                  pl.BlockSpec((tq, 128), lambda b, h, i: (rb0 + b * nq + i, 0)),
                  pl.BlockSpec((None, lk, 2 * (dn + dv)), lambda b, h, i: (b, 0, h)),
                  pl.BlockSpec((None, lk, 128), lambda b, h, i: (b, 0, 0))],
        out_specs=pl.BlockSpec((tq, 2 * dv), lambda b, h, i: (b * nq + i, h)),
        out_shape=jax.ShapeDtypeStruct((nb * lq, heads * dv), BF16),
        compiler_params=_cparams(3), name=f"mla_attn_{lq}",
    )(q, q, q, cosq, sinq, kv, kpe)


def _mla(tok, x, mods, mod_base, gnorm, w_dq, q_norm_g, w_uq, w_dkv, kv_norm_g, w_ukv, w_o, cache_ckv, cache_kpe):
    d = x.shape[1]
    heads = MLA_HEADS
    qr_, kvr, dr = w_dq.shape[1], kv_norm_g.shape[0], cache_kpe.shape[-1]
    dv = w_o.shape[0] // heads
    dn = w_ukv.shape[1] // heads - dv
    past = cache_ckv.shape[1]
    tm = tok.row_tile()
    w_a = jnp.concatenate([w_dq, w_dkv, _rot_cols(w_dkv[:, kvr:], dr)], axis=1)
    wq = w_uq.reshape(qr_, heads, dn + dr)
    wq_pe = wq[:, :, dn:]
    w_q = jnp.concatenate([wq[:, :, :dn].reshape(qr_, heads * dn), wq_pe.reshape(qr_, heads * dr),
                           _rot_cols(wq_pe, dr).reshape(qr_, heads * dr)], axis=1)
    qkv = _matmul(x, w_a, tm=tm, tn=_pick(w_a.shape[1], (384, 256, 128)), pre="normmod", g=gnorm, mods=mods,
                  mod_base=mod_base, tok=tok, shift_idx=0, scale_idx=1, name="mla_down")
    cos, sin = _rope_tables(tok.ls, dr)
    cos2 = jnp.tile(jnp.concatenate([cos, cos], axis=1), (tok.bs, 1))
    sin2 = jnp.tile(jnp.concatenate([sin, sin], axis=1), (tok.bs, 1))
    ones, zeros = jnp.ones((tok.tp, dr), F32), jnp.zeros((tok.tp, dr), F32)
    cos_t = jnp.concatenate([ones, cos2], axis=0)
    sin_t = jnp.concatenate([zeros, sin2], axis=0)
    ckv, kpe = _mla_kv(qkv, qr_, kvr, dr, kv_norm_g, jnp.concatenate([cos_t, sin_t], axis=1), tm)
    q = _matmul(qkv, w_q, tm=tm, tn=_pick(w_q.shape[1], (512, 256, 128)), tk=qr_, pre="norm", g=q_norm_g,
                name="mla_q")
    cosq = jnp.concatenate([cos_t, cos_t], axis=1)
    sinq = jnp.concatenate([sin_t, sin_t], axis=1)
    ckv_s = jnp.concatenate([cache_ckv, ckv[tok.tp:].reshape(tok.bs, tok.ls, kvr)], axis=1)
    kpe_s = jnp.concatenate([jnp.tile(cache_kpe, (1, 1, 2)), kpe[tok.tp:].reshape(tok.bs, tok.ls, 2 * dr)], axis=1)
    lk_s = past + tok.ls
    ckv_all = jnp.concatenate([ckv[:tok.tp], ckv_s.reshape(tok.bs * lk_s, kvr)], axis=0)
    kv = _matmul(ckv_all, w_ukv, tm=_pick(math.gcd(tok.tp, tok.bs * lk_s), (1024, 512, 256, 128)),
                 tn=_pick(w_ukv.shape[1], (512, 256, 128)), out_dtype=BF16, name="mla_up_kv")
    nkv = kv.shape[1]
    o_p = _attention(q, cosq, sinq, kv[:tok.tp].reshape(tok.bp, tok.lp, nkv),
                     kpe[:tok.tp].reshape(tok.bp, tok.lp, 2 * dr), 0, tok.bp, tok.lp, tok.lp, heads, dn, dr, dv)
    o_s = _attention(q, cosq, sinq, kv[tok.tp:].reshape(tok.bs, lk_s, nkv), kpe_s, tok.tp, tok.bs, tok.ls, lk_s,
                     heads, dn, dr, dv)
    o = jnp.concatenate([o_p, o_s], axis=0)
    out = _matmul(o, w_o, tm=tm, tn=_pick(d, (512, 256, 128)), epi="resgate", res=x, mods=mods,
                  mod_base=mod_base, tok=tok, gate_idx=2, name="mla_out")
    new_ckv = ckv[:tok.tp].reshape(tok.bp, tok.lp, kvr)
    new_kpe = qkv[:tok.tp, qr_ + kvr:qr_ + kvr + dr].reshape(tok.bp, tok.lp, dr)
    return out, new_ckv, new_kpe


def _k_tile(kdim, cap):
    return max(c for c in range(LANES, cap + 1, LANES) if kdim % c == 0)


def _dense_ffn(tok, x, mods, mod_base, gnorm, lyr, w_gate, w_up, w_down):
    _, d, dff = w_gate.shape
    tm = tok.row_tile()
    hmid = _matmul(x, w_gate, w2=w_up, w_lead=(lyr,), tm=tm, tn=_pick(dff, (512, 256, 128)), pre="normmod", g=gnorm,
                   mods=mods, mod_base=mod_base, tok=tok, shift_idx=3, scale_idx=4, out_dtype=BF16, name="ffn_up")
    return _matmul(hmid, w_down, w_lead=(lyr,), tm=tm, tn=_pick(d, (512, 256, 128)), tk=_k_tile(dff, 3072),
                   x_resident=True, epi="resgate", res=x, mods=mods, mod_base=mod_base, tok=tok, gate_idx=5, name="ffn_down")


def _route_body(x_ref, g_ref, sh_ref, sc_ref, wr_ref, br_ref, h_ref, info_ref, cnt_ref, carry_ref, *, n_exp):
    i = pl.program_id(0)
    tm, d = x_ref.shape
    slab = d // LANES

    @pl.when(i == 0)
    def _():
        carry_ref[...] = jnp.zeros_like(carry_ref)

    xf = x_ref[...]
    ms = jnp.mean(xf * xf, axis=-1, keepdims=True)
    hh = ((xf * lax.rsqrt(ms + EPS)) * g_ref[...]) * (1.0 + sc_ref[0]) + sh_ref[0]
    for s in range(slab):
        h_ref[pl.ds(s, tm, stride=slab), :] = hh[:, s * LANES:(s + 1) * LANES]
    logits = jnp.dot(hh, wr_ref[...], preferred_element_type=F32, precision=HIGHEST) + br_ref[...]
    lane = lax.broadcasted_iota(jnp.int32, logits.shape, 1)
    logits = jnp.where(lane < n_exp, logits, -jnp.inf)
    t1 = jnp.max(logits, axis=-1, keepdims=True)
    i1 = jnp.min(jnp.where(logits == t1, lane, 128), axis=-1, keepdims=True)
    rest = jnp.where(lane == i1, -jnp.inf, logits)
    t2 = jnp.max(rest, axis=-1, keepdims=True)
    i2 = jnp.min(jnp.where(rest == t2, lane, 128), axis=-1, keepdims=True)
    e2 = jnp.exp(t2 - t1)
    g1 = 1.0 / (1.0 + e2)
    g2 = e2 / (1.0 + e2)
    oh1 = lane == i1
    oh2 = lane == i2
    cnt = jnp.where(oh1 | oh2, 1.0, 0.0).astype(BF16)
    ri = lax.broadcasted_iota(jnp.int32, (tm, tm), 0)
    ci = lax.broadcasted_iota(jnp.int32, (tm, tm), 1)
    before = _dot((ci < ri).astype(BF16), cnt) + carry_ref[...]
    r1 = jnp.sum(jnp.where(oh1, before, 0.0), axis=-1, keepdims=True)
    r2 = jnp.sum(jnp.where(oh2, before, 0.0), axis=-1, keepdims=True)
    carry_ref[...] += jnp.sum(cnt.astype(F32), axis=0, keepdims=True)
    vals = (i1.astype(F32), i2.astype(F32), r1, r2, g1, g2)
    info = jnp.zeros(logits.shape, F32)
    for q, v in enumerate(vals):
        info = jnp.where(lane == q, v, info)
    info_ref[...] = info
    cnt_ref[...] = carry_ref[...]


def _moe_route(tok, x, mods, mod_base, gnorm, w_router, b_router, tm):
    t, d = x.shape
    n_exp = w_router.shape[1]
    wr = jnp.pad(w_router, ((0, 0), (0, 128 - n_exp)))
    br = jnp.pad(b_router, (0, 128 - n_exp)).reshape(1, 128)
    slab = d // LANES
    modspec = lambda idx: pl.BlockSpec((1, 1, d), lambda i: (mod_base + tok.group_of(i * tm) * 6 + idx, 0, 0))
    return pl.pallas_call(
        functools.partial(_route_body, n_exp=n_exp), grid=(t // tm,),
        in_specs=[pl.BlockSpec((tm, d), lambda i: (i, 0)), pl.BlockSpec((1, d), lambda i: (0, 0)),
                  modspec(3), modspec(4),
                  pl.BlockSpec((d, 128), lambda i: (0, 0)), pl.BlockSpec((1, 128), lambda i: (0, 0))],
        out_specs=[pl.BlockSpec((tm * slab, LANES), lambda i: (i, 0)),
                   pl.BlockSpec((tm, 128), lambda i: (i, 0)),
                   pl.BlockSpec((1, 128), lambda i: (0, 0))],
        out_shape=[jax.ShapeDtypeStruct((t * slab, LANES), F32), jax.ShapeDtypeStruct((t, 128), F32),
                   jax.ShapeDtypeStruct((1, 128), F32)],
        scratch_shapes=[pltpu.VMEM((1, 128), F32)],
        compiler_params=_cparams(1), name="moe_route",
    )(x, gnorm.reshape(1, d), mods, mods, wr, br)


def _dispatch_body(dest_ref, h_ref, init_hbm, o_hbm, sem, *, tm, slab):
    del init_hbm
    base = pl.program_id(0) * tm

    def copies(r):
        src = h_ref.at[pl.ds(r * slab, slab)]
        return [pltpu.make_async_copy(src, o_hbm.at[pl.ds(dest_ref[2 * (base + r) + k] * slab, slab)], sem)
                for k in range(2)]

    def start(r, c):
        for cp in copies(r):
            cp.start()
        return c

    def wait(r, c):
        for cp in copies(r):
            cp.wait()
        return c

    lax.fori_loop(0, tm, start, 0)
    lax.fori_loop(0, tm, wait, 0)


def _moe_dispatch(h_slab, dest, n_rows, slab, tm):
    t = h_slab.shape[0] // slab
    init = jnp.zeros((n_rows * slab, LANES), F32)
    grid_spec = pltpu.PrefetchScalarGridSpec(
        num_scalar_prefetch=1, grid=(t // tm,),
        in_specs=[pl.BlockSpec((tm * slab, LANES), lambda i, p: (i, 0)), pl.BlockSpec(memory_space=pl.ANY)],
        out_specs=pl.BlockSpec(memory_space=pl.ANY),
        scratch_shapes=[pltpu.SemaphoreType.DMA(())])
    return pl.pallas_call(
        functools.partial(_dispatch_body, tm=tm, slab=slab), grid_spec=grid_spec,
        out_shape=jax.ShapeDtypeStruct(init.shape, F32), input_output_aliases={2: 0},
        compiler_params=_cparams(1), name="moe_dispatch",
    )(dest, h_slab, init)


def _combine_body(pos_ref, x_ref, gate_ref, info_ref, y_hbm, o_ref, buf1, buf2, sem, *, tm):
    base = pl.program_id(0) * tm
    slab = buf1.shape[0] // tm

    def copies(r):
        rows = pl.ds(r * slab, slab)
        src = lambda k: y_hbm.at[pl.ds(pos_ref[2 * (base + r) + k] * slab, slab)]
        return (pltpu.make_async_copy(src(0), buf1.at[rows], sem),
                pltpu.make_async_copy(src(1), buf2.at[rows], sem))

    def start(r, c):
        for cp in copies(r):
            cp.start()
        return c

    def wait(r, c):
        for cp in copies(r):
            cp.wait()
        return c

    lax.fori_loop(0, tm, start, 0)
    lax.fori_loop(0, tm, wait, 0)
    g1 = info_ref[:, 4:5]
    g2 = info_ref[:, 5:6]
    for s in range(slab):
        cols = slice(s * LANES, (s + 1) * LANES)
        rows = pl.ds(s, tm, stride=slab)
        mix = buf1[rows, :] * g1 + buf2[rows, :] * g2
        o_ref[:, cols] = x_ref[:, cols] + gate_ref[0][:, cols] * mix


def _moe_combine(tok, x, mods, mod_base, info, y_slab, pos, tm):
    t, d = x.shape
    slab = d // LANES
    grid_spec = pltpu.PrefetchScalarGridSpec(
        num_scalar_prefetch=1, grid=(t // tm,),
        in_specs=[pl.BlockSpec((tm, d), lambda i, p: (i, 0)),
                  pl.BlockSpec((1, 1, d), lambda i, p: (mod_base + tok.group_of(i * tm) * 6 + 5, 0, 0)),
                  pl.BlockSpec((tm, 128), lambda i, p: (i, 0)),
                  pl.BlockSpec(memory_space=pl.ANY)],
        out_specs=pl.BlockSpec((tm, d), lambda i, p: (i, 0)),
        scratch_shapes=[pltpu.VMEM((tm * slab, LANES), F32), pltpu.VMEM((tm * slab, LANES), F32),
                        pltpu.SemaphoreType.DMA(())])
    return pl.pallas_call(
        functools.partial(_combine_body, tm=tm), grid_spec=grid_spec,
        out_shape=jax.ShapeDtypeStruct((t, d), F32),
        compiler_params=_cparams(1), name="moe_combine",
    )(pos, x, mods, info, y_slab)


def _moe_ffn(tok, x, mods, mod_base, gnorm, lyr, w_router, b_router, w_gate, w_up, w_down):
    t, d = x.shape
    _, n_exp, _, dexp = w_gate.shape
    tm = tok.row_tile()
    h_slab, info, counts = _moe_route(tok, x, mods, mod_base, gnorm, w_router, b_router, tm)
    te = _pick(2 * t, (1024, 512, 256, 128, 64, 32, 16, 8))
    counts = counts[0, :n_exp].astype(jnp.int32)
    padded = (counts + te - 1) // te * te
    pad_end = jnp.cumsum(padded)
    pad_start = pad_end - padded
    idx = info[:, 0:2].astype(jnp.int32)
    rank = info[:, 2:4].astype(jnp.int32)
    pos = (pad_start[idx] + rank).reshape(-1)
    n_tiles = 2 * t // te + n_exp
    n_rows = n_tiles * te
    used = pad_end[-1] // te
    tile_row = jnp.minimum(jnp.arange(n_tiles, dtype=jnp.int32), used - 1)
    tile_exp = jnp.minimum(jnp.searchsorted(pad_end, tile_row * te, side="right"), n_exp - 1).astype(jnp.int32)
    tmc = _pick(tok.row_tile(), (256, 128, 64, 32, 16, 8))
    xs = _moe_dispatch(h_slab, pos, n_rows, d // LANES, tmc)
    hmid = _matmul(xs, w_gate, w2=w_up, w_lead=(lyr,), tm=te, tn=_pick(dexp, (512, 256, 128)),
                   tiles=(tile_exp, tile_row), slab_in=True, n_rows=n_rows, out_dtype=BF16, name="moe_up")
    y_slab = _matmul(hmid, w_down, w_lead=(lyr,), tm=te, tn=_pick(d, (512, 256, 128)), tk=_k_tile(dexp, 2048),
                     tiles=(tile_exp, tile_row), slab_out=True, x_resident=True, n_rows=n_rows, name="moe_down")
    return _moe_combine(tok, x, mods, mod_base, info, y_slab, pos, tmc)


def kernel(x_prompt, x_sample, c, cache_mla_ckv, cache_mla_kpe, state_ssd_fwd, state_ssd_bwd, c_ctx, norm1_g, norm2_g, w_mod, b_mod, norm_f_g, hy_w_in, hy_conv_w, hy_conv_b, hy_f_w1, hy_f_b1, hy_f_w2, hy_f_b2, hy_f_w3, hy_decay, hy_skip, hy_w_out, ssd_w_in, ssd_conv_w, ssd_conv_b, ssd_dt_bias, ssd_a_log, ssd_d, ssd_norm_g, ssd_w_out, mla_w_dq, mla_q_norm_g, mla_w_uq, mla_w_dkv, mla_kv_norm_g, mla_w_ukv, mla_w_o, ffn_w_gate, ffn_w_up, ffn_w_down, moe_w_router, moe_b_router, moe_w_gate, moe_w_up, moe_w_down):
    bp, lp, d = x_prompt.shape
    bs, ls, _ = x_sample.shape
    depth = w_mod.shape[0]
    tok = _Tokens(bp, lp, bs, ls)
    x = jnp.concatenate([x_prompt.reshape(tok.tp, d), x_sample.reshape(tok.ts, d)], axis=0)

    gp = -(-tok.groups // 8) * 8
    cond = jnp.concatenate([c_ctx[None, :], c, jnp.zeros((gp - tok.groups, d), F32)], axis=0)
    mods = _modulation(cond, w_mod, b_mod).reshape(depth * gp * 6, 1, d)

    ckv_new, kpe_new, sf_new, sb_new = [], [], [], []
    for i in range(depth):
        base = i * gp * 6
        j, kind = i // N_MIXERS, i % N_MIXERS
        if kind == 0:
            x = _hyena(tok, x, mods, base, norm1_g[i], j, hy_w_in, hy_conv_w[j], hy_conv_b[j], hy_f_w1[j],
                       hy_f_b1[j], hy_f_w2[j], hy_f_b2[j], hy_f_w3[j], hy_decay[j], hy_skip[j], hy_w_out)
        elif kind == 1:
            x, (s_f, s_b) = _ssd(tok, x, mods, base, norm1_g[i], ssd_w_in[j], ssd_conv_w[j], ssd_conv_b[j],
                                 ssd_dt_bias[j], ssd_a_log[j], ssd_d[j], ssd_norm_g[j], ssd_w_out[j],
                                 state_ssd_fwd[:, j], state_ssd_bwd[:, j])
            sf_new.append(s_f)
            sb_new.append(s_b)
        else:
            x, ckv, kpe = _mla(tok, x, mods, base, norm1_g[i], mla_w_dq[j], mla_q_norm_g[j], mla_w_uq[j],
                               mla_w_dkv[j], mla_kv_norm_g[j], mla_w_ukv[j], mla_w_o[j],
                               cache_mla_ckv[:, j], cache_mla_kpe[:, j])
            ckv_new.append(ckv)
            kpe_new.append(kpe)
        k = i // 2
        if i % 2 == 0:
            x = _dense_ffn(tok, x, mods, base, norm2_g[i], k, ffn_w_gate, ffn_w_up, ffn_w_down)
        else:
            x = _moe_ffn(tok, x, mods, base, norm2_g[i], k, moe_w_router[k], moe_b_router[k], moe_w_gate,
                         moe_w_up, moe_w_down)
    y = _rmsnorm(x, norm_f_g, tok.row_tile(512))
    y_prompt = y[:tok.tp].reshape(bp, lp, d)
    y_sample = y[tok.tp:].reshape(bs, ls, d)
    return (y_prompt, y_sample, jnp.stack(ckv_new, axis=1), jnp.stack(kpe_new, axis=1),
            jnp.stack(sf_new, axis=1), jnp.stack(sb_new, axis=1))
```

```python
import functools
import math

import jax
import jax.numpy as jnp
from jax import lax
from jax.experimental import pallas as pl
from jax.experimental.pallas import tpu as pltpu

F32 = jnp.float32
BF16 = jnp.bfloat16
EPS = 1e-6
N_MIXERS = 3
MLA_HEADS = 16
GRID_W = 64
ROPE_BASE = 10000.0
HY_BANDS = 16
HY_SIN_FREQ = 1.0
SSD_CHUNK = 128
LANES = 128
VMEM_LIMIT_BYTES = 56 * 1024 * 1024
HIGHEST = lax.Precision.HIGHEST


def _cparams(n_axes):
    return pltpu.CompilerParams(dimension_semantics=("arbitrary",) * n_axes,
                                vmem_limit_bytes=VMEM_LIMIT_BYTES)


def _pick(n, candidates):
    for c in candidates:
        if n % c == 0:
            return c
    raise ValueError(f"no tile for {n} in {candidates}")


def _silu(x):
    return x * (1.0 / (1.0 + jnp.exp(-x)))


def _dot(a, b):
    return jnp.dot(a, b, preferred_element_type=F32)


def _dot_nt(a, b):
    return lax.dot_general(a, b, (((1,), (1,)), ((), ())), preferred_element_type=F32)


def _dot_tn(a, b):
    return lax.dot_general(a, b, (((0,), (0,)), ((), ())), preferred_element_type=F32)


class _Tokens:
    def __init__(self, bp, lp, bs, ls):
        self.bp, self.lp, self.bs, self.ls = bp, lp, bs, ls
        self.tp, self.ts = bp * lp, bs * ls
        self.t = self.tp + self.ts
        self.groups = 1 + bs
        assert self.tp % ls == 0 or self.tp % lp == 0

    def row_tile(self, cap=1024):
        return _pick(math.gcd(self.tp, self.ls), [c for c in (1024, 512, 256, 128, 64, 32, 16, 8) if c <= cap])

    def group_of(self, row_start):
        return jnp.where(row_start < self.tp, 0, 1 + (row_start - self.tp) // self.ls)


def _mm_body(*refs, pre, swiglu, epi, nk, nj, grouped, slab_in, slab_out, precise, tn):
    refs = list(refs)
    tile_row_ref = None
    if grouped:
        refs.pop(0)
        tile_row_ref = refs.pop(0)
    x_ref = refs.pop(0)
    g_ref = refs.pop(0) if pre else None
    sh_ref = sc_ref = None
    if pre == "normmod":
        sh_ref = refs.pop(0)
        sc_ref = refs.pop(0)
    w_ref = refs.pop(0)
    w2_ref = refs.pop(0) if swiglu else None
    res_ref = gate_ref = None
    if epi == "resgate":
        res_ref = refs.pop(0)
        gate_ref = refs.pop(0)
    o_ref = refs.pop(0)
    h_scr = refs.pop(0) if (pre or slab_in) else None
    acc_ref = refs.pop(0) if nk > 1 else None
    i = pl.program_id(0)
    j = pl.program_id(1)
    k = pl.program_id(2)

    def prologue():
        if slab_in:
            tm, kdim = h_scr.shape
            slab = kdim // LANES
            for s in range(slab):
                h_scr[:, s * LANES:(s + 1) * LANES] = x_ref[pl.ds(s, tm, stride=slab), :].astype(BF16)
        else:
            xf = x_ref[...].astype(F32)
            ms = jnp.mean(xf * xf, axis=-1, keepdims=True)
            y = (xf * lax.rsqrt(ms + EPS)) * g_ref[...]
            if pre == "normmod":
                y = y * (1.0 + sc_ref[0]) + sh_ref[0]
            h_scr[...] = y.astype(BF16)

    def product(wr):
        if precise:
            xf, wf = x_ref[...], wr[...]
            xh, wh = xf.astype(BF16), wf.astype(BF16)
            xl = (xf - xh.astype(F32)).astype(BF16)
            wl = (wf - wh.astype(F32)).astype(BF16)
            return _dot(xh, wh) + (_dot(xh, wl) + _dot(xl, wh))
        if pre or slab_in:
            xb = h_scr[...]
        elif x_ref.shape[1] != wr.shape[0]:
            tk = wr.shape[0]
            xb = x_ref[:, pl.ds(pl.multiple_of(k * tk, tk), tk)].astype(BF16)
        else:
            xb = x_ref[...].astype(BF16)
        return _dot(xb, wr[...].astype(BF16))

    def store_slab(a, s0):
        slab = (nj * tn) // LANES
        tm = o_ref.shape[0] // slab
        for q in range(tn // LANES):
            o_ref[pl.ds(s0 + q, tm, stride=slab), :] = a[:, q * LANES:(q + 1) * LANES].astype(o_ref.dtype)

    def finish(a, a2):
        if swiglu:
            a = _silu(a) * a2
        if epi == "resgate":
            a = res_ref[...] + gate_ref[0] * a
        if slab_out:
            for jj in range(nj):
                pl.when(j == jj)(functools.partial(store_slab, a, jj * (tn // LANES)))
        else:
            o_ref[...] = a.astype(o_ref.dtype)

    def compute():
        if pre or slab_in:
            pl.when(j == 0)(prologue)
        if nk > 1:
            @pl.when(k == 0)
            def _():
                acc_ref[...] = jnp.zeros_like(acc_ref)
        a = product(w_ref)
        a2 = product(w2_ref) if swiglu else None
        if nk == 1:
            finish(a, a2)
        else:
            acc_ref[...] += a

            @pl.when(k == nk - 1)
            def _():
                finish(acc_ref[...], None)

    if grouped:
        used = tile_row_ref[i] == i
        pl.when(used)(compute)

        @pl.when(jnp.logical_not(used))
        def _():
            o_ref[...] = jnp.zeros_like(o_ref)
    else:
        compute()


def _matmul(x, w, *, tm, tn, tk=None, w2=None, pre=None, g=None, mods=None, mod_base=None, tok=None,
            shift_idx=None, scale_idx=None, epi=None, res=None, gate_idx=None, out_dtype=F32,
            x_col0=0, n_rows=None, row0=0, w_col0=0, n_cols=None, w_lead=(), tiles=None, slab_in=False,
            slab_out=False, x_resident=False, precise=False, name="mm"):
    grouped = tiles is not None
    kdim = w.shape[-2]
    n = (w.shape[-1] - w_col0) if n_cols is None else n_cols
    tk = kdim if tk is None else tk
    nk = kdim // tk
    assert kdim % tk == 0 and n % tn == 0 and w_col0 % tn == 0
    wcb0 = w_col0 // tn
    swiglu = w2 is not None
    assert nk == 1 or not (pre or swiglu)
    x_resident = x_resident and nk > 1
    if slab_in:
        assert x.shape[1] == LANES and kdim % LANES == 0 and nk == 1 and not pre
        rows = x.shape[0] // (kdim // LANES) if n_rows is None else n_rows
    else:
        rows = (x.shape[0] - row0) if n_rows is None else n_rows
    assert rows % tm == 0 and row0 % tm == 0
    ni, nj = (tiles[0].shape[0] if grouped else rows // tm), n // tn
    rb0 = row0 // tm
    xcb0 = x_col0 // tk
    assert x_col0 % tk == 0

    def rowblk(i, pref):
        return pref[1][i] if grouped else i + rb0

    def colblk(i, j, pref):
        return jnp.where(pref[1][i] == i, j, nj - 1) if grouped else j

    def kblk(i, k, pref):
        return jnp.where(pref[1][i] == i, k, nk - 1) if grouped else k

    def grp(i):
        return tok.group_of(i * tm)

    in_specs, args = [], []
    once = dict(pipeline_mode=pl.Buffered(1)) if x_resident else {}
    if slab_in:
        in_specs.append(pl.BlockSpec((tm * (kdim // LANES), LANES), lambda i, j, k, *p: (rowblk(i, p), 0)))
    elif x_resident:
        assert x_col0 == 0 and x.shape[1] == kdim and nk > 1
        in_specs.append(pl.BlockSpec((tm, kdim), lambda i, j, k, *p: (rowblk(i, p), 0), **once))
    else:
        in_specs.append(pl.BlockSpec((tm, tk), lambda i, j, k, *p: (rowblk(i, p), xcb0 + kblk(i, k, p)), **once))
    args.append(x)
    if pre:
        in_specs.append(pl.BlockSpec((1, tk), lambda i, j, k, *p: (0, 0)))
        args.append(g.reshape(1, tk).astype(F32))
    if pre == "normmod":
        for idx in (shift_idx, scale_idx):
            in_specs.append(pl.BlockSpec((1, 1, tk), lambda i, j, k, *p, idx=idx: (mod_base + grp(i) * 6 + idx, 0, 0)))
            args.append(mods)
    lead = tuple(w_lead)
    none = (None,) * len(lead)
    wspec = (pl.BlockSpec(none + (None, tk, tn),
                          lambda i, j, k, *p: lead + (p[0][i], kblk(i, k, p), wcb0 + colblk(i, j, p))) if grouped
             else pl.BlockSpec(none + (tk, tn), lambda i, j, k, *p: lead + (k, wcb0 + j)))
    in_specs.append(wspec)
    args.append(w)
    if swiglu:
        in_specs.append(wspec)
        args.append(w2)
    if epi == "resgate":
        in_specs.append(pl.BlockSpec((tm, tn), lambda i, j, k, *p: (i + rb0, j)))
        args.append(res)
        in_specs.append(pl.BlockSpec((1, 1, tn), lambda i, j, k, *p: (mod_base + grp(i) * 6 + gate_idx, 0, j)))
        args.append(mods)
    if slab_out:
        assert tn % LANES == 0
        out_shape = jax.ShapeDtypeStruct((rows * (n // LANES), LANES), out_dtype)
        out_spec = pl.BlockSpec((tm * (n // LANES), LANES), lambda i, j, k, *p: (i, 0))
    else:
        out_shape = jax.ShapeDtypeStruct((rows, n), out_dtype)
        out_spec = pl.BlockSpec((tm, tn), lambda i, j, k, *p: (i, j))
    scratch = []
    if pre or slab_in:
        scratch.append(pltpu.VMEM((tm, tk), BF16))
    if nk > 1:
        scratch.append(pltpu.VMEM((tm, tn), F32))
    body = functools.partial(_mm_body, pre=pre, swiglu=swiglu, epi=epi, nk=nk, nj=nj, grouped=grouped,
                             slab_in=slab_in, slab_out=slab_out, precise=precise, tn=tn)
    grid_spec = pltpu.PrefetchScalarGridSpec(
        num_scalar_prefetch=2 if grouped else 0, grid=(ni, nj, nk),
        in_specs=in_specs, out_specs=out_spec, scratch_shapes=scratch)
    call = pl.pallas_call(body, grid_spec=grid_spec, out_shape=out_shape,
                          compiler_params=_cparams(3), name=name)
    return call(*tiles, *args) if grouped else call(*args)


def _mod_body(c_ref, w_ref, b_ref, o_ref):
    cb = _silu(c_ref[...]).astype(BF16)
    o_ref[...] = _dot(cb, w_ref[...].astype(BF16)) + b_ref[...]


def _modulation(cond, w_mod, b_mod):
    depth, d, n = w_mod.shape
    gp = cond.shape[0]
    tn = _pick(n, (1024, 512, 256, 128))
    return pl.pallas_call(
        _mod_body,
        grid=(depth, n // tn),
        in_specs=[pl.BlockSpec((gp, d), lambda i, j: (0, 0)),
                  pl.BlockSpec((None, d, tn), lambda i, j: (i, 0, j)),
                  pl.BlockSpec((None, 1, tn), lambda i, j: (i, 0, j))],
        out_specs=pl.BlockSpec((None, gp, tn), lambda i, j: (i, 0, j)),
        out_shape=jax.ShapeDtypeStruct((depth, gp, n), F32),
        compiler_params=_cparams(2), name="modulation",
    )(cond, w_mod, b_mod.reshape(depth, 1, n))


def _rms_body(x_ref, g_ref, o_ref):
    xf = x_ref[...]
    ms = jnp.mean(xf * xf, axis=-1, keepdims=True)
    o_ref[...] = (xf * lax.rsqrt(ms + EPS)) * g_ref[...]


def _rmsnorm(x, g, tm):
    t, d = x.shape
    return pl.pallas_call(
        _rms_body, grid=(t // tm,),
        in_specs=[pl.BlockSpec((tm, d), lambda i: (i, 0)), pl.BlockSpec((1, d), lambda i: (0, 0))],
        out_specs=pl.BlockSpec((tm, d), lambda i: (i, 0)),
        out_shape=jax.ShapeDtypeStruct((t, d), F32),
        compiler_params=_cparams(1), name="final_norm",
    )(x, g.reshape(1, d))


def _dft_tables(l):
    f = jnp.arange(l, dtype=jnp.int32)
    m = (f[:, None] * f[None, :]) % (2 * l)
    ang = m.astype(F32) * (math.pi / l)
    return jnp.cos(ang), jnp.sin(ang)


def _hy_features(l):
    t = jnp.arange(l, dtype=F32)
    t01 = t / (l - 1)
    bands = jnp.linspace(1e-4, HY_BANDS - 1, HY_BANDS, dtype=F32)
    ang = (2.0 * math.pi / l) * t[:, None] * bands[None, :]
    z = jnp.concatenate([t01[:, None], jnp.cos(ang), jnp.sin(ang)], axis=-1)
    return jnp.pad(z, ((0, 0), (0, 128 - z.shape[1])))


def _hy_taps_body(z_ref, w1_ref, b1_ref, w2_ref, b2_ref, w3f_ref, w3b_ref, dec_ref, ksum_ref, kdif_ref, knyq_ref):
    l = z_ref.shape[0]
    dot = functools.partial(jnp.dot, preferred_element_type=F32, precision=HIGHEST)
    f = jnp.sin(HY_SIN_FREQ * (dot(z_ref[...], w1_ref[...]) + b1_ref[...]))
    f = jnp.sin(HY_SIN_FREQ * (dot(f, w2_ref[...]) + b2_ref[...]))
    row = lax.broadcasted_iota(jnp.int32, (l, 1), 0)
    t01 = row.astype(F32) / (l - 1)
    kf = dot(f, w3f_ref[...]) * jnp.exp(-t01 * dec_ref[0:1, :])
    kb = dot(f, w3b_ref[...]) * jnp.exp(-t01 * dec_ref[1:2, :])
    kb = jnp.where(row == 0, 0.0, kb)
    l1 = jnp.sum(jnp.abs(kf), axis=0, keepdims=True) + jnp.sum(jnp.abs(kb), axis=0, keepdims=True)
    kf = kf / l1
    kb = kb / l1
    ks = kf + kb
    alt = jnp.where(row % 2 == 0, 1.0, -1.0)
    ksum_ref[...] = ks
    kdif_ref[...] = kb - kf
    knyq_ref[...] = jnp.sum(ks * alt, axis=0, keepdims=True)


def _hy_taps(l, w1, b1, w2, b2, w3, decay):
    d = decay.shape[1]
    nf = w2.shape[0]
    cb = _pick(d, (512, 256, 128))
    z = _hy_features(l)
    w1p = jnp.pad(w1, ((0, 128 - w1.shape[0]), (0, 0)))
    full = lambda shape: pl.BlockSpec(shape, lambda c: (0,) * len(shape))
    return pl.pallas_call(
        _hy_taps_body, grid=(d // cb,),
        in_specs=[full((l, 128)), full((128, nf)), full((1, nf)), full((nf, nf)), full((1, nf)),
                  pl.BlockSpec((nf, cb), lambda c: (0, c)),
                  pl.BlockSpec((nf, cb), lambda c: (0, d // cb + c)),
                  pl.BlockSpec((2, cb), lambda c: (0, c))],
        out_specs=[pl.BlockSpec((l, cb), lambda c: (0, c)), pl.BlockSpec((l, cb), lambda c: (0, c)),
                   pl.BlockSpec((1, cb), lambda c: (0, c))],
        out_shape=[jax.ShapeDtypeStruct((l, d), F32), jax.ShapeDtypeStruct((l, d), F32),
                   jax.ShapeDtypeStruct((1, d), F32)],
        compiler_params=_cparams(1), name="hyena_taps",
    )(z, w1p, b1.reshape(1, nf), w2, b2.reshape(1, nf), w3, w3, decay)


def _dwconv3(a, w_ref, b_ref, row, l):
    prev = jnp.where(row == 0, 0.0, pltpu.roll(a, 1, axis=0))
    nxt = jnp.where(row == l - 1, 0.0, pltpu.roll(a, l - 1, axis=0))
    return prev * w_ref[0:1, :] + a * w_ref[1:2, :] + nxt * w_ref[2:3, :] + b_ref[...]


def _hy_core_body(x0_ref, x1_ref, v_ref, w0_ref, w1_ref, wv_ref, b0_ref, b1_ref, bv_ref, skip_ref,
                  kr_ref, ki_ref, knyq_ref, c_ref, s_ref, o_ref):
    l = x0_ref.shape[0]
    row = lax.broadcasted_iota(jnp.int32, (l, 1), 0)
    x0 = _dwconv3(x0_ref[...], w0_ref, b0_ref, row, l)
    x1 = _dwconv3(x1_ref[...], w1_ref, b1_ref, row, l)
    v = _dwconv3(v_ref[...], wv_ref, bv_ref, row, l)
    z = v * x1
    zb = z.astype(BF16)
    cm = c_ref[...]
    sm = s_ref[...]
    zr = _dot(cm, zb)
    zs = _dot(sm, zb)
    wf = jnp.where(row == 0, 0.5 / l, 1.0 / l)
    kr = kr_ref[...] * wf
    ki = ki_ref[...] * wf
    a = (zr * kr + zs * ki).astype(BF16)
    b = (zs * kr - zr * ki).astype(BF16)
    alt = jnp.where(row % 2 == 0, 1.0, -1.0)
    znyq = jnp.sum(z * alt, axis=0, keepdims=True)
    y = _dot(cm, a) + _dot(sm, b) + alt * (znyq * knyq_ref[...] * (0.5 / l))
    o_ref[...] = (x0 * (y + z * skip_ref[...])).astype(o_ref.dtype)


def _hy_core(u, row0, nseq, l, d, conv_w, conv_b, skip, kr, ki, knyq, cmat, smat):
    cb = _pick(d, (128,) if l > 512 else (512, 256, 128))
    nd = d // cb
    sb0 = row0 // l
    assert row0 % l == 0
    useq = lambda sec: pl.BlockSpec((l, cb), lambda s, c, sec=sec: (sb0 + s, sec * nd + c))
    wsec = lambda sec: pl.BlockSpec((3, cb), lambda s, c, sec=sec: (0, sec * nd + c))
    bsec = lambda sec: pl.BlockSpec((1, cb), lambda s, c, sec=sec: (0, sec * nd + c))
    col = lambda rows: pl.BlockSpec((rows, cb), lambda s, c: (0, c))
    const = pl.BlockSpec((l, l), lambda s, c: (0, 0), pipeline_mode=pl.Buffered(1))
    cb3 = conv_b.reshape(1, 3 * d)
    return pl.pallas_call(
        _hy_core_body, grid=(nseq, nd),
        in_specs=[useq(0), useq(1), useq(2), wsec(0), wsec(1), wsec(2), bsec(0), bsec(1), bsec(2),
                  col(1), col(l), col(l), col(1), const, const],
        out_specs=pl.BlockSpec((l, cb), lambda s, c: (s, c)),
        out_shape=jax.ShapeDtypeStruct((nseq * l, d), BF16),
        compiler_params=_cparams(2), name=f"hyena_core_{l}",
    )(u, u, u, conv_w, conv_w, conv_w, cb3, cb3, cb3, skip.reshape(1, d), kr, ki, knyq, cmat, smat)


def _hyena(tok, x, mods, mod_base, g, lyr, w_in, conv_w, conv_b, f_w1, f_b1, f_w2, f_b2, f_w3, decay, skip, w_out):
    d = x.shape[1]
    tm = tok.row_tile()
    u = _matmul(x, w_in, w_lead=(lyr,), tm=tm, tn=_pick(3 * d, (512, 256, 128)), pre="normmod", g=g, mods=mods,
                mod_base=mod_base, tok=tok, shift_idx=0, scale_idx=1, name="hyena_in")
    parts = []
    for row0, nseq, l in ((0, tok.bp, tok.lp), (tok.tp, tok.bs, tok.ls)):
        ksum, kdif, knyq = _hy_taps(l, f_w1, f_b1, f_w2, f_b2, f_w3, decay)
        cmat, smat = _dft_tables(l)
        tf = _pick(l, (512, 256, 128))
        tn = _pick(d, (512, 256, 128))
        kr = _matmul(cmat, ksum, tm=tf, tn=tn, precise=True, name="hyena_spec_re")
        ki = _matmul(smat, kdif, tm=tf, tn=tn, precise=True, name="hyena_spec_im")
        parts.append(_hy_core(u, row0, nseq, l, d, conv_w, conv_b, skip, kr, ki, knyq,
                              cmat.astype(BF16), smat.astype(BF16)))
    gmix = jnp.concatenate(parts, axis=0)
    return _matmul(gmix, w_out, w_lead=(lyr,), tm=tm, tn=_pick(d, (512, 256, 128)), epi="resgate", res=x,
                   mods=mods, mod_base=mod_base, tok=tok, gate_idx=2, name="hyena_out")


def _ssd_conv_body(x_ref, w_ref, b_ref, o_ref):
    l = x_ref.shape[0]
    row = lax.broadcasted_iota(jnp.int32, (l, 1), 0)
    o_ref[...] = _silu(_dwconv3(x_ref[...], w_ref, b_ref, row, l))


def _ssd_conv(proj, col0, conv_w, conv_b, row0, nseq, l):
    cdim = conv_w.shape[1]
    cb = _pick(math.gcd(cdim, col0), (512, 256, 128))
    cb0, sb0 = col0 // cb, row0 // l
    return pl.pallas_call(
        _ssd_conv_body, grid=(nseq, cdim // cb),
        in_specs=[pl.BlockSpec((l, cb), lambda s, c: (sb0 + s, cb0 + c)),
                  pl.BlockSpec((3, cb), lambda s, c: (0, c)),
                  pl.BlockSpec((1, cb), lambda s, c: (0, c))],
        out_specs=pl.BlockSpec((l, cb), lambda s, c: (s, c)),
        out_shape=jax.ShapeDtypeStruct((nseq * l, cdim), F32),
        compiler_params=_cparams(2), name=f"ssd_conv_{l}",
    )(proj, conv_w, conv_b.reshape(1, cdim))


def _ssd_dt_body(raw_ref, bias_ref, alog_ref, dt_ref, cum_ref):
    tm, w = raw_ref.shape
    xr = raw_ref[...] + bias_ref[...]
    dt = jnp.maximum(xr, 0.0) + jnp.log1p(jnp.exp(-jnp.abs(xr)))
    dt_ref[...] = dt
    a = dt * (-jnp.exp(alog_ref[...]))
    ri = lax.broadcasted_iota(jnp.int32, (SSD_CHUNK, SSD_CHUNK), 0)
    ci = lax.broadcasted_iota(jnp.int32, (SSD_CHUNK, SSD_CHUNK), 1)
    lower = (ci <= ri).astype(F32)
    upper = (ci >= ri).astype(F32)
    lane = lax.broadcasted_iota(jnp.int32, (SSD_CHUNK, w), 1)
    for q in range(tm // SSD_CHUNK):
        ac = a[q * SSD_CHUNK:(q + 1) * SSD_CHUNK, :]
        pc = jnp.dot(lower, ac, preferred_element_type=F32, precision=HIGHEST)
        rc = jnp.dot(upper, ac, preferred_element_type=F32, precision=HIGHEST)
        cum_ref[q * SSD_CHUNK:(q + 1) * SSD_CHUNK, :] = jnp.where(lane < w // 2, pc, rc)


def _ssd_dt(proj, col0, dt_bias, a_log, tm):
    t = proj.shape[0]
    w = dt_bias.size
    assert col0 % w == 0
    return pl.pallas_call(
        _ssd_dt_body, grid=(t // tm,),
        in_specs=[pl.BlockSpec((tm, w), lambda i: (i, col0 // w)),
                  pl.BlockSpec((1, w), lambda i: (0, 0)), pl.BlockSpec((1, w), lambda i: (0, 0))],
        out_specs=[pl.BlockSpec((tm, w), lambda i: (i, 0)), pl.BlockSpec((tm, w), lambda i: (i, 0))],
        out_shape=[jax.ShapeDtypeStruct((t, w), F32), jax.ShapeDtypeStruct((t, w), F32)],
        compiler_params=_cparams(1), name="ssd_dt",
    )(proj, dt_bias.reshape(1, w), a_log.reshape(1, w))


def _ssd_scan_body(*refs, has_init, r, p, nc):
    refs = list(refs)
    x_ref, b_ref, c_ref, dsk_ref = refs[:4]
    dt_refs, cum_refs, cumt_refs = refs[4:6], refs[6:8], refs[8:10]
    refs = refs[10:]
    init_refs = (refs.pop(0), refs.pop(0)) if has_init else None
    y_ref, fin_refs, st_ref = refs[0], refs[1:3], refs[3]
    ch = SSD_CHUNK
    npair = r // 2
    lane = lax.broadcasted_iota(jnp.int32, (ch, 2 * p), 1)
    left = lane < p
    ri = lax.broadcasted_iota(jnp.int32, (ch, ch), 0)
    ci = lax.broadcasted_iota(jnp.int32, (ch, ch), 1)

    for d in range(2):
        keep = (ci <= ri) if d == 0 else (ci >= ri)
        for pr in range(npair):
            if has_init:
                st_ref[pr] = init_refs[d][pr * 2 * p:(pr + 1) * 2 * p, :].T
            else:
                st_ref[pr] = jnp.zeros((st_ref.shape[1], 2 * p), F32)

        def chunk(ic, carry, d=d, keep=keep):
            cidx = ic if d == 0 else nc - 1 - ic
            r0 = pl.multiple_of(cidx * ch, ch)
            rows = pl.ds(r0, ch)
            bc = b_ref[rows, :].astype(BF16)
            cc = c_ref[rows, :].astype(BF16)
            cbm = _dot_nt(cc, bc)
            bt = b_ref[rows, :].T.astype(BF16)
            dtc = dt_refs[d][rows, :]
            cumc = cum_refs[d][rows, :]
            cumt = cumt_refs[d][:, rows]
            edge = cumc[0:1, :] if d == 1 else cumc[ch - 1:ch, :]
            for pr in range(npair):
                xpair = x_ref[rows, pr * 2 * p:(pr + 1) * 2 * p]
                ha, hb = 2 * pr, 2 * pr + 1
                sel = lambda col: jnp.where(left, col[:, ha:ha + 1], col[:, hb:hb + 1])
                xs = xpair * sel(dtc)
                xsb = xs.astype(BF16)
                ydiag = []
                for h in (ha, hb):
                    seg = cumc[:, h:h + 1] - cumt[h:h + 1, :]
                    dec = jnp.exp(jnp.where(keep, seg, -jnp.inf))
                    ydiag.append(_dot((cbm * dec).astype(BF16), xsb))
                cum_pair = sel(cumc)
                st = st_ref[pr]
                yoff = _dot(cc, st.astype(BF16)) * jnp.exp(cum_pair)
                yc = jnp.where(left, ydiag[0], ydiag[1]) + yoff
                edge_pair = jnp.where(left[0:1, :], edge[:, ha:ha + 1], edge[:, hb:hb + 1])
                xsd = (xs * jnp.exp(edge_pair - cum_pair)).astype(BF16)
                st_ref[pr] = st * jnp.exp(edge_pair) + _dot(bt, xsd)
                cols = slice(pr * 2 * p, (pr + 1) * 2 * p)
                if d == 0:
                    y_ref[rows, cols] = yc + xpair * dsk_ref[:, cols]
                else:
                    y_ref[rows, cols] += yc
            return carry

        lax.fori_loop(0, nc, chunk, 0)
        for pr in range(npair):
            fin_refs[d][pr * 2 * p:(pr + 1) * 2 * p, :] = st_ref[pr].T


def _ssd_scan(xbc, dtg, cumg, cumtg, d_vec, nseq, l, di, g, n, p, init):
    h = di // p
    r = h // g
    rp = r * p
    assert r % 2 == 0 and rp % 128 == 0 and n % 128 == 0 and l % SSD_CHUNK == 0
    nb0, nc0 = di // n, (di + g * n) // n
    has_init = init is not None
    dspec = lambda d: pl.BlockSpec((None, l, r), lambda s, gi, d=d: (d * g + gi, s, 0))
    tspec = lambda d: pl.BlockSpec((None, r, l), lambda s, gi, d=d: (d * g + gi, 0, s))
    sspec = pl.BlockSpec((None, rp, n), lambda s, gi: (s, gi, 0))
    in_specs = [pl.BlockSpec((l, rp), lambda s, gi: (s, gi)),
                pl.BlockSpec((l, n), lambda s, gi: (s, nb0 + gi)),
                pl.BlockSpec((l, n), lambda s, gi: (s, nc0 + gi)),
                pl.BlockSpec((1, rp), lambda s, gi: (0, gi)),
                dspec(0), dspec(1), dspec(0), dspec(1), tspec(0), tspec(1)]
    args = [xbc, xbc, xbc, d_vec, dtg, dtg, cumg, cumg, cumtg, cumtg]
    if has_init:
        in_specs += [sspec, sspec]
        args += [s.reshape(nseq, h * p, n) for s in init]
    st_shape = jax.ShapeDtypeStruct((nseq, h * p, n), F32)
    body = functools.partial(_ssd_scan_body, has_init=has_init, r=r, p=p, nc=l // SSD_CHUNK)
    y, sf, sb = pl.pallas_call(
        body, grid=(nseq, g), in_specs=in_specs,
        out_specs=[pl.BlockSpec((l, rp), lambda s, gi: (s, gi)), sspec, sspec],
        out_shape=[jax.ShapeDtypeStruct((nseq * l, di), F32), st_shape, st_shape],
        scratch_shapes=[pltpu.VMEM((r // 2, n, 2 * p), F32)],
        compiler_params=_cparams(2), name=f"ssd_scan_{l}",
    )(*args)
    return y, sf.reshape(nseq, h, p, n), sb.reshape(nseq, h, p, n)


def _ssd_gate_body(y_ref, z_ref, g_ref, o_ref):
    v = y_ref[...] * _silu(z_ref[...])
    ms = jnp.mean(v * v, axis=-1, keepdims=True)
    o_ref[...] = ((v * lax.rsqrt(ms + EPS)) * g_ref[...]).astype(o_ref.dtype)


def _ssd_gate(y, proj, norm_g, tm):
    t, di = y.shape
    return pl.pallas_call(
        _ssd_gate_body, grid=(t // tm,),
        in_specs=[pl.BlockSpec((tm, di), lambda i: (i, 0)), pl.BlockSpec((tm, di), lambda i: (i, 0)),
                  pl.BlockSpec((1, di), lambda i: (0, 0))],
        out_specs=pl.BlockSpec((tm, di), lambda i: (i, 0)),
        out_shape=jax.ShapeDtypeStruct((t, di), BF16),
        compiler_params=_cparams(1), name="ssd_gate",
    )(y, proj, norm_g.reshape(1, di))


def _ssd(tok, x, mods, mod_base, gnorm, w_in, conv_w, conv_b, dt_bias, a_log, d_skip, norm_g, w_out,
         init_f, init_b):
    d = x.shape[1]
    p, n = init_f.shape[-2], init_f.shape[-1]
    h = dt_bias.shape[1]
    di = h * p
    cdim = conv_w.shape[1]
    g = (cdim - di) // (2 * n)
    r = h // g
    tm = tok.row_tile()
    pre_kw = dict(pre="normmod", g=gnorm, mods=mods, mod_base=mod_base, tok=tok, shift_idx=0, scale_idx=1)
    proj = _matmul(x, w_in, tm=tm, tn=_pick(math.gcd(di, cdim), (512, 256, 128)), n_cols=di + cdim,
                   name="ssd_in", **pre_kw)
    dt_raw = _matmul(x, w_in, tm=tm, tn=2 * h, w_col0=di + cdim, name="ssd_in_dt", **pre_kw)
    dt, cum = _ssd_dt(dt_raw, 0, dt_bias, a_log, _pick(tok.t, (512, 256, 128)))
    split = lambda a: jnp.transpose(a.reshape(tok.t, 2 * g, r), (1, 0, 2))
    dtg, cumg = split(dt), split(cum)
    cumtg = jnp.transpose(cumg, (0, 2, 1))
    d_vec = jnp.repeat(d_skip.astype(F32), p).reshape(1, di)
    ys, states = [], []
    for row0, nseq, l, init in ((0, tok.bp, tok.lp, None), (tok.tp, tok.bs, tok.ls, (init_f, init_b))):
        xbc = _ssd_conv(proj, di, conv_w, conv_b, row0, nseq, l)
        sl = slice(row0, row0 + nseq * l)
        y, sf, sb = _ssd_scan(xbc, dtg[:, sl], cumg[:, sl], cumtg[:, :, sl], d_vec, nseq, l, di, g, n, p, init)
        ys.append(y)
        states.append((sf, sb))
    yg = _ssd_gate(jnp.concatenate(ys, axis=0), proj, norm_g, _pick(tok.t, (256, 128)))
    out = _matmul(yg, w_out, tm=tm, tn=_pick(d, (512, 256, 128)), tk=_pick(di, (2048, 1024, 512, 256)),
                  epi="resgate", res=x, mods=mods, mod_base=mod_base, tok=tok, gate_idx=2, name="ssd_out")
    return out, states[0]


def _rope_tables(n_tokens, dr):
    axis_dim = dr // 2
    rows = n_tokens // GRID_W
    row = jnp.broadcast_to(jnp.arange(rows, dtype=F32)[:, None], (rows, GRID_W)).reshape(-1)
    col = jnp.broadcast_to(jnp.arange(GRID_W, dtype=F32)[None, :], (rows, GRID_W)).reshape(-1)
    inv_freq = ROPE_BASE ** (-jnp.arange(0, axis_dim, 2, dtype=F32) / axis_dim)
    ang = jnp.concatenate([row[:, None] * inv_freq, col[:, None] * inv_freq], axis=-1)
    return jnp.cos(ang), jnp.sin(ang)


def _rot_cols(w, dr):
    half = dr // 2
    return jnp.concatenate([-w[..., half:], w[..., :half]], axis=-1)


def _mla_kv_body(kv_ref, pe_ref, g_ref, cs_ref, ckv_ref, kpe_ref):
    kv = kv_ref[...]
    ms = jnp.mean(kv * kv, axis=-1, keepdims=True)
    ckv_ref[...] = (kv * lax.rsqrt(ms + EPS)) * g_ref[...]
    prod = pe_ref[...] * cs_ref[...]
    kpe_ref[...] = prod + pltpu.roll(prod, prod.shape[1] // 2, axis=1)


def _mla_kv(qkv, col0, rank, dr, g, cs, tm):
    t = qkv.shape[0]
    assert col0 % rank == 0 and (col0 + rank) % (2 * dr) == 0 and 2 * dr == 128
    return pl.pallas_call(
        _mla_kv_body, grid=(t // tm,),
        in_specs=[pl.BlockSpec((tm, rank), lambda i: (i, col0 // rank)),
                  pl.BlockSpec((tm, 2 * dr), lambda i: (i, (col0 + rank) // (2 * dr))),
                  pl.BlockSpec((1, rank), lambda i: (0, 0)),
                  pl.BlockSpec((tm, 2 * dr), lambda i: (i, 0))],
        out_specs=[pl.BlockSpec((tm, rank), lambda i: (i, 0)), pl.BlockSpec((tm, 2 * dr), lambda i: (i, 0))],
        out_shape=[jax.ShapeDtypeStruct((t, rank), F32), jax.ShapeDtypeStruct((t, 2 * dr), F32)],
        compiler_params=_cparams(1), name="mla_kv",
    )(qkv, qkv, g.reshape(1, rank), cs)


def _attn_body(qn_ref, qp_ref, qr_ref, cos_ref, sin_ref, kv_ref, kpe_ref, o_ref, *, dn, dr, dv, scale):
    qrot = qp_ref[...] * cos_ref[...] + qr_ref[...] * sin_ref[...]
    lane = lax.broadcasted_iota(jnp.int32, qrot.shape, 1)
    kpe = kpe_ref[...].astype(BF16)
    for h in range(2):
        qn = qn_ref[:, h * dn:(h + 1) * dn].astype(BF16)
        kn = kv_ref[:, h * (dn + dv):h * (dn + dv) + dn]
        v = kv_ref[:, h * (dn + dv) + dn:(h + 1) * (dn + dv)]
        qh = jnp.where((lane >= h * dr) & (lane < (h + 1) * dr), qrot, 0.0).astype(BF16)
        s = _dot_nt(jnp.concatenate([qn, qh], axis=1), jnp.concatenate([kn, kpe], axis=1))
        m = jnp.max(s, axis=-1, keepdims=True)
        e = jnp.exp((s - m) * scale)
        o = _dot(e.astype(BF16), v) / jnp.sum(e, axis=-1, keepdims=True)
        o_ref[:, h * dv:(h + 1) * dv] = o.astype(o_ref.dtype)


def _attention(q, cosq, sinq, kv, kpe, row0, nb, lq, lk, heads, dn, dr, dv):
    assert dn == 128 and dv == 128 and dr == 64 and heads % 2 == 0
    tq = _pick(lq, (512, 256, 128))
    nq = lq // tq
    rb0 = row0 // tq
    hp = heads // 2
    pe0, rot0 = heads * dn // 128, (heads * dn + heads * dr) // 128
    body = functools.partial(_attn_body, dn=dn, dr=dr, dv=dv, scale=(dn + dr) ** -0.5)
    return pl.pallas_call(
        body, grid=(nb, hp, nq),
        in_specs=[pl.BlockSpec((tq, 2 * dn), lambda b, h, i: (rb0 + b * nq + i, h)),
                  pl.BlockSpec((tq, 128), lambda b, h, i: (rb0 + b * nq + i, pe0 + h)),
                  pl.BlockSpec((tq, 128), lambda b, h, i: (rb0 + b * nq + i, rot0 + h)),
                  pl.BlockSpec((tq, 128), lambda b, h, i: (rb0 + b * nq + i, 0)),
                  pl.BlockSpec((tq, 128), lambda b, h, i: (rb0 + b * nq + i, 0)),
                  pl.BlockSpec((None, lk, 2 * (dn + dv)), lambda b, h, i: (b, 0, h)),
                  pl.BlockSpec((None, lk, 128), lambda b, h, i: (b, 0, 0))],
        out_specs=pl.BlockSpec((tq, 2 * dv), lambda b, h, i: (b * nq + i, h)),
        out_shape=jax.ShapeDtypeStruct((nb * lq, heads * dv), BF16),
        compiler_params=_cparams(3), name=f"mla_attn_{lq}",
    )(q, q, q, cosq, sinq, kv, kpe)


def _mla(tok, x, mods, mod_base, gnorm, w_dq, q_norm_g, w_uq, w_dkv, kv_norm_g, w_ukv, w_o, cache_ckv, cache_kpe):
    d = x.shape[1]
    heads = MLA_HEADS
    qr_, kvr, dr = w_dq.shape[1], kv_norm_g.shape[0], cache_kpe.shape[-1]
    dv = w_o.shape[0] // heads
    dn = w_ukv.shape[1] // heads - dv
    past = cache_ckv.shape[1]
    tm = tok.row_tile()
    w_a = jnp.concatenate([w_dq, w_dkv, _rot_cols(w_dkv[:, kvr:], dr)], axis=1)
    wq = w_uq.reshape(qr_, heads, dn + dr)
    wq_pe = wq[:, :, dn:]
    w_q = jnp.concatenate([wq[:, :, :dn].reshape(qr_, heads * dn), wq_pe.reshape(qr_, heads * dr),
                           _rot_cols(wq_pe, dr).reshape(qr_, heads * dr)], axis=1)
    qkv = _matmul(x, w_a, tm=tm, tn=_pick(w_a.shape[1], (384, 256, 128)), pre="normmod", g=gnorm, mods=mods,
                  mod_base=mod_base, tok=tok, shift_idx=0, scale_idx=1, name="mla_down")
    cos, sin = _rope_tables(tok.ls, dr)
    cos2 = jnp.tile(jnp.concatenate([cos, cos], axis=1), (tok.bs, 1))
    sin2 = jnp.tile(jnp.concatenate([sin, sin], axis=1), (tok.bs, 1))
    ones, zeros = jnp.ones((tok.tp, dr), F32), jnp.zeros((tok.tp, dr), F32)
    cos_t = jnp.concatenate([ones, cos2], axis=0)
    sin_t = jnp.concatenate([zeros, sin2], axis=0)
    ckv, kpe = _mla_kv(qkv, qr_, kvr, dr, kv_norm_g, jnp.concatenate([cos_t, sin_t], axis=1), tm)
    q = _matmul(qkv, w_q, tm=tm, tn=_pick(w_q.shape[1], (512, 256, 128)), tk=qr_, pre="norm", g=q_norm_g,
                name="mla_q")
    cosq = jnp.concatenate([cos_t, cos_t], axis=1)
    sinq = jnp.concatenate([sin_t, sin_t], axis=1)
    ckv_s = jnp.concatenate([cache_ckv, ckv[tok.tp:].reshape(tok.bs, tok.ls, kvr)], axis=1)
    kpe_s = jnp.concatenate([jnp.tile(cache_kpe, (1, 1, 2)), kpe[tok.tp:].reshape(tok.bs, tok.ls, 2 * dr)], axis=1)
    lk_s = past + tok.ls
    ckv_all = jnp.concatenate([ckv[:tok.tp], ckv_s.reshape(tok.bs * lk_s, kvr)], axis=0)
    kv = _matmul(ckv_all, w_ukv, tm=_pick(math.gcd(tok.tp, tok.bs * lk_s), (1024, 512, 256, 128)),
                 tn=_pick(w_ukv.shape[1], (512, 256, 128)), out_dtype=BF16, name="mla_up_kv")
    nkv = kv.shape[1]
    o_p = _attention(q, cosq, sinq, kv[:tok.tp].reshape(tok.bp, tok.lp, nkv),
                     kpe[:tok.tp].reshape(tok.bp, tok.lp, 2 * dr), 0, tok.bp, tok.lp, tok.lp, heads, dn, dr, dv)
    o_s = _attention(q, cosq, sinq, kv[tok.tp:].reshape(tok.bs, lk_s, nkv), kpe_s, tok.tp, tok.bs, tok.ls, lk_s,
                     heads, dn, dr, dv)
    o = jnp.concatenate([o_p, o_s], axis=0)
    out = _matmul(o, w_o, tm=tm, tn=_pick(d, (512, 256, 128)), epi="resgate", res=x, mods=mods,
                  mod_base=mod_base, tok=tok, gate_idx=2, name="mla_out")
    new_ckv = ckv[:tok.tp].reshape(tok.bp, tok.lp, kvr)
    new_kpe = qkv[:tok.tp, qr_ + kvr:qr_ + kvr + dr].reshape(tok.bp, tok.lp, dr)
    return out, new_ckv, new_kpe


def _k_tile(kdim, cap):
    return max(c for c in range(LANES, cap + 1, LANES) if kdim % c == 0)


def _dense_ffn(tok, x, mods, mod_base, gnorm, lyr, w_gate, w_up, w_down):
    _, d, dff = w_gate.shape
    tm = tok.row_tile()
    hmid = _matmul(x, w_gate, w2=w_up, w_lead=(lyr,), tm=tm, tn=_pick(dff, (512, 256, 128)), pre="normmod", g=gnorm,
                   mods=mods, mod_base=mod_base, tok=tok, shift_idx=3, scale_idx=4, out_dtype=BF16, name="ffn_up")
    return _matmul(hmid, w_down, w_lead=(lyr,), tm=tm, tn=_pick(d, (512, 256, 128)), tk=_k_tile(dff, 3072),
                   x_resident=True, epi="resgate", res=x, mods=mods, mod_base=mod_base, tok=tok, gate_idx=5, name="ffn_down")


def _route_body(x_ref, g_ref, sh_ref, sc_ref, wr_ref, br_ref, h_ref, info_ref, cnt_ref, carry_ref, *, n_exp):
    i = pl.program_id(0)
    tm, d = x_ref.shape
    slab = d // LANES

    @pl.when(i == 0)
    def _():
        carry_ref[...] = jnp.zeros_like(carry_ref)

    xf = x_ref[...]
    ms = jnp.mean(xf * xf, axis=-1, keepdims=True)
    hh = ((xf * lax.rsqrt(ms + EPS)) * g_ref[...]) * (1.0 + sc_ref[0]) + sh_ref[0]
    for s in range(slab):
        h_ref[pl.ds(s, tm, stride=slab), :] = hh[:, s * LANES:(s + 1) * LANES]
    logits = jnp.dot(hh, wr_ref[...], preferred_element_type=F32, precision=HIGHEST) + br_ref[...]
    lane = lax.broadcasted_iota(jnp.int32, logits.shape, 1)
    logits = jnp.where(lane < n_exp, logits, -jnp.inf)
    t1 = jnp.max(logits, axis=-1, keepdims=True)
    i1 = jnp.min(jnp.where(logits == t1, lane, 128), axis=-1, keepdims=True)
    rest = jnp.where(lane == i1, -jnp.inf, logits)
    t2 = jnp.max(rest, axis=-1, keepdims=True)
    i2 = jnp.min(jnp.where(rest == t2, lane, 128), axis=-1, keepdims=True)
    e2 = jnp.exp(t2 - t1)
    g1 = 1.0 / (1.0 + e2)
    g2 = e2 / (1.0 + e2)
    oh1 = lane == i1
    oh2 = lane == i2
    cnt = jnp.where(oh1 | oh2, 1.0, 0.0).astype(BF16)
    ri = lax.broadcasted_iota(jnp.int32, (tm, tm), 0)
    ci = lax.broadcasted_iota(jnp.int32, (tm, tm), 1)
    before = _dot((ci < ri).astype(BF16), cnt) + carry_ref[...]
    r1 = jnp.sum(jnp.where(oh1, before, 0.0), axis=-1, keepdims=True)
    r2 = jnp.sum(jnp.where(oh2, before, 0.0), axis=-1, keepdims=True)
    carry_ref[...] += jnp.sum(cnt.astype(F32), axis=0, keepdims=True)
    vals = (i1.astype(F32), i2.astype(F32), r1, r2, g1, g2)
    info = jnp.zeros(logits.shape, F32)
    for q, v in enumerate(vals):
        info = jnp.where(lane == q, v, info)
    info_ref[...] = info
    cnt_ref[...] = carry_ref[...]


def _moe_route(tok, x, mods, mod_base, gnorm, w_router, b_router, tm):
    t, d = x.shape
    n_exp = w_router.shape[1]
    wr = jnp.pad(w_router, ((0, 0), (0, 128 - n_exp)))
    br = jnp.pad(b_router, (0, 128 - n_exp)).reshape(1, 128)
    slab = d // LANES
    modspec = lambda idx: pl.BlockSpec((1, 1, d), lambda i: (mod_base + tok.group_of(i * tm) * 6 + idx, 0, 0))
    return pl.pallas_call(
        functools.partial(_route_body, n_exp=n_exp), grid=(t // tm,),
        in_specs=[pl.BlockSpec((tm, d), lambda i: (i, 0)), pl.BlockSpec((1, d), lambda i: (0, 0)),
                  modspec(3), modspec(4),
                  pl.BlockSpec((d, 128), lambda i: (0, 0)), pl.BlockSpec((1, 128), lambda i: (0, 0))],
        out_specs=[pl.BlockSpec((tm * slab, LANES), lambda i: (i, 0)),
                   pl.BlockSpec((tm, 128), lambda i: (i, 0)),
                   pl.BlockSpec((1, 128), lambda i: (0, 0))],
        out_shape=[jax.ShapeDtypeStruct((t * slab, LANES), F32), jax.ShapeDtypeStruct((t, 128), F32),
                   jax.ShapeDtypeStruct((1, 128), F32)],
        scratch_shapes=[pltpu.VMEM((1, 128), F32)],
        compiler_params=_cparams(1), name="moe_route",
    )(x, gnorm.reshape(1, d), mods, mods, wr, br)


def _dispatch_body(dest_ref, h_ref, init_hbm, o_hbm, sem, *, tm, slab):
    del init_hbm
    base = pl.program_id(0) * tm

    def copies(r):
        src = h_ref.at[pl.ds(r * slab, slab)]
        return [pltpu.make_async_copy(src, o_hbm.at[pl.ds(dest_ref[2 * (base + r) + k] * slab, slab)], sem)
                for k in range(2)]

    def start(r, c):
        for cp in copies(r):
            cp.start()
        return c

    def wait(r, c):
        for cp in copies(r):
            cp.wait()
        return c

    lax.fori_loop(0, tm, start, 0)
    lax.fori_loop(0, tm, wait, 0)


def _moe_dispatch(h_slab, dest, n_rows, slab, tm):
    t = h_slab.shape[0] // slab
    init = jnp.zeros((n_rows * slab, LANES), F32)
    grid_spec = pltpu.PrefetchScalarGridSpec(
        num_scalar_prefetch=1, grid=(t // tm,),
        in_specs=[pl.BlockSpec((tm * slab, LANES), lambda i, p: (i, 0)), pl.BlockSpec(memory_space=pl.ANY)],
        out_specs=pl.BlockSpec(memory_space=pl.ANY),
        scratch_shapes=[pltpu.SemaphoreType.DMA(())])
    return pl.pallas_call(
        functools.partial(_dispatch_body, tm=tm, slab=slab), grid_spec=grid_spec,
        out_shape=jax.ShapeDtypeStruct(init.shape, F32), input_output_aliases={2: 0},
        compiler_params=_cparams(1), name="moe_dispatch",
    )(dest, h_slab, init)


def _combine_body(pos_ref, x_ref, gate_ref, info_ref, y_hbm, o_ref, buf1, buf2, sem, *, tm):
    base = pl.program_id(0) * tm
    slab = buf1.shape[0] // tm

    def copies(r):
        rows = pl.ds(r * slab, slab)
        src = lambda k: y_hbm.at[pl.ds(pos_ref[2 * (base + r) + k] * slab, slab)]
        return (pltpu.make_async_copy(src(0), buf1.at[rows], sem),
                pltpu.make_async_copy(src(1), buf2.at[rows], sem))

    def start(r, c):
        for cp in copies(r):
            cp.start()
        return c

    def wait(r, c):
        for cp in copies(r):
            cp.wait()
        return c

    lax.fori_loop(0, tm, start, 0)
    lax.fori_loop(0, tm, wait, 0)
    g1 = info_ref[:, 4:5]
    g2 = info_ref[:, 5:6]
    for s in range(slab):
        cols = slice(s * LANES, (s + 1) * LANES)
        rows = pl.ds(s, tm, stride=slab)
        mix = buf1[rows, :] * g1 + buf2[rows, :] * g2
        o_ref[:, cols] = x_ref[:, cols] + gate_ref[0][:, cols] * mix


def _moe_combine(tok, x, mods, mod_base, info, y_slab, pos, tm):
    t, d = x.shape
    slab = d // LANES
    grid_spec = pltpu.PrefetchScalarGridSpec(
        num_scalar_prefetch=1, grid=(t // tm,),
        in_specs=[pl.BlockSpec((tm, d), lambda i, p: (i, 0)),
                  pl.BlockSpec((1, 1, d), lambda i, p: (mod_base + tok.group_of(i * tm) * 6 + 5, 0, 0)),
                  pl.BlockSpec((tm, 128), lambda i, p: (i, 0)),
                  pl.BlockSpec(memory_space=pl.ANY)],
        out_specs=pl.BlockSpec((tm, d), lambda i, p: (i, 0)),
        scratch_shapes=[pltpu.VMEM((tm * slab, LANES), F32), pltpu.VMEM((tm * slab, LANES), F32),
                        pltpu.SemaphoreType.DMA(())])
    return pl.pallas_call(
        functools.partial(_combine_body, tm=tm), grid_spec=grid_spec,
        out_shape=jax.ShapeDtypeStruct((t, d), F32),
        compiler_params=_cparams(1), name="moe_combine",
    )(pos, x, mods, info, y_slab)


def _moe_ffn(tok, x, mods, mod_base, gnorm, lyr, w_router, b_router, w_gate, w_up, w_down):
    t, d = x.shape
    _, n_exp, _, dexp = w_gate.shape
    tm = tok.row_tile()
    h_slab, info, counts = _moe_route(tok, x, mods, mod_base, gnorm, w_router, b_router, tm)
    te = _pick(2 * t, (1024, 512, 256, 128, 64, 32, 16, 8))
    counts = counts[0, :n_exp].astype(jnp.int32)
    padded = (counts + te - 1) // te * te
    pad_end = jnp.cumsum(padded)
    pad_start = pad_end - padded
    idx = info[:, 0:2].astype(jnp.int32)
    rank = info[:, 2:4].astype(jnp.int32)
    pos = (pad_start[idx] + rank).reshape(-1)
    n_tiles = 2 * t // te + n_exp
    n_rows = n_tiles * te
    used = pad_end[-1] // te
    tile_row = jnp.minimum(jnp.arange(n_tiles, dtype=jnp.int32), used - 1)
    tile_exp = jnp.minimum(jnp.searchsorted(pad_end, tile_row * te, side="right"), n_exp - 1).astype(jnp.int32)
    tmc = _pick(tok.row_tile(), (256, 128, 64, 32, 16, 8))
    xs = _moe_dispatch(h_slab, pos, n_rows, d // LANES, tmc)
    hmid = _matmul(xs, w_gate, w2=w_up, w_lead=(lyr,), tm=te, tn=_pick(dexp, (512, 256, 128)),
                   tiles=(tile_exp, tile_row), slab_in=True, n_rows=n_rows, out_dtype=BF16, name="moe_up")
    y_slab = _matmul(hmid, w_down, w_lead=(lyr,), tm=te, tn=_pick(d, (512, 256, 128)), tk=_k_tile(dexp, 2048),
                     tiles=(tile_exp, tile_row), slab_out=True, x_resident=True, n_rows=n_rows, name="moe_down")
    return _moe_combine(tok, x, mods, mod_base, info, y_slab, pos, tmc)


def kernel(x_prompt, x_sample, c, cache_mla_ckv, cache_mla_kpe, state_ssd_fwd, state_ssd_bwd, c_ctx, norm1_g, norm2_g, w_mod, b_mod, norm_f_g, hy_w_in, hy_conv_w, hy_conv_b, hy_f_w1, hy_f_b1, hy_f_w2, hy_f_b2, hy_f_w3, hy_decay, hy_skip, hy_w_out, ssd_w_in, ssd_conv_w, ssd_conv_b, ssd_dt_bias, ssd_a_log, ssd_d, ssd_norm_g, ssd_w_out, mla_w_dq, mla_q_norm_g, mla_w_uq, mla_w_dkv, mla_kv_norm_g, mla_w_ukv, mla_w_o, ffn_w_gate, ffn_w_up, ffn_w_down, moe_w_router, moe_b_router, moe_w_gate, moe_w_up, moe_w_down):
    bp, lp, d = x_prompt.shape
    bs, ls, _ = x_sample.shape
    depth = w_mod.shape[0]
    tok = _Tokens(bp, lp, bs, ls)
    x = jnp.concatenate([x_prompt.reshape(tok.tp, d), x_sample.reshape(tok.ts, d)], axis=0)

    gp = -(-tok.groups // 8) * 8
    cond = jnp.concatenate([c_ctx[None, :], c, jnp.zeros((gp - tok.groups, d), F32)], axis=0)
    mods = _modulation(cond, w_mod, b_mod).reshape(depth * gp * 6, 1, d)

    ckv_new, kpe_new, sf_new, sb_new = [], [], [], []
    for i in range(depth):
        base = i * gp * 6
        j, kind = i // N_MIXERS, i % N_MIXERS
        if kind == 0:
            x = _hyena(tok, x, mods, base, norm1_g[i], j, hy_w_in, hy_conv_w[j], hy_conv_b[j], hy_f_w1[j],
                       hy_f_b1[j], hy_f_w2[j], hy_f_b2[j], hy_f_w3[j], hy_decay[j], hy_skip[j], hy_w_out)
        elif kind == 1:
            x, (s_f, s_b) = _ssd(tok, x, mods, base, norm1_g[i], ssd_w_in[j], ssd_conv_w[j], ssd_conv_b[j],
                                 ssd_dt_bias[j], ssd_a_log[j], ssd_d[j], ssd_norm_g[j], ssd_w_out[j],
                                 state_ssd_fwd[:, j], state_ssd_bwd[:, j])
            sf_new.append(s_f)
            sb_new.append(s_b)
        else:
            x, ckv, kpe = _mla(tok, x, mods, base, norm1_g[i], mla_w_dq[j], mla_q_norm_g[j], mla_w_uq[j],
                               mla_w_dkv[j], mla_kv_norm_g[j], mla_w_ukv[j], mla_w_o[j],
                               cache_mla_ckv[:, j], cache_mla_kpe[:, j])
            ckv_new.append(ckv)
            kpe_new.append(kpe)
        k = i // 2
        if i % 2 == 0:
            x = _dense_ffn(tok, x, mods, base, norm2_g[i], k, ffn_w_gate, ffn_w_up, ffn_w_down)
        else:
            x = _moe_ffn(tok, x, mods, base, norm2_g[i], k, moe_w_router[k], moe_b_router[k], moe_w_gate,
                         moe_w_up, moe_w_down)
    y = _rmsnorm(x, norm_f_g, tok.row_tile(512))
    y_prompt = y[:tok.tp].reshape(bp, lp, d)
    y_sample = y[tok.tp:].reshape(bs, ls, d)
    return (y_prompt, y_sample, jnp.stack(ckv_new, axis=1), jnp.stack(kpe_new, axis=1),
            jnp.stack(sf_new, axis=1), jnp.stack(sb_new, axis=1))
```

```python
import functools
import math

import jax
import jax.numpy as jnp
from jax import lax
from jax.experimental import pallas as pl
from jax.experimental.pallas import tpu as pltpu

F32 = jnp.float32
BF16 = jnp.bfloat16
EPS = 1e-6
N_MIXERS = 3
MLA_HEADS = 16
GRID_W = 64
ROPE_BASE = 10000.0
HY_BANDS = 16
HY_SIN_FREQ = 1.0
SSD_CHUNK = 128
LANES = 128
VMEM_LIMIT_BYTES = 56 * 1024 * 1024
HIGHEST = lax.Precision.HIGHEST


def _cparams(n_axes):
    return pltpu.CompilerParams(dimension_semantics=("arbitrary",) * n_axes,
                                vmem_limit_bytes=VMEM_LIMIT_BYTES)


def _pick(n, candidates):
    for c in candidates:
        if n % c == 0:
            return c
    raise ValueError(f"no tile for {n} in {candidates}")


def _silu(x):
    return x * (1.0 / (1.0 + jnp.exp(-x)))


def _dot(a, b):
    return jnp.dot(a, b, preferred_element_type=F32)


def _dot_nt(a, b):
    return lax.dot_general(a, b, (((1,), (1,)), ((), ())), preferred_element_type=F32)


def _dot_tn(a, b):
    return lax.dot_general(a, b, (((0,), (0,)), ((), ())), preferred_element_type=F32)


class _Tokens:
    def __init__(self, bp, lp, bs, ls):
        self.bp, self.lp, self.bs, self.ls = bp, lp, bs, ls
        self.tp, self.ts = bp * lp, bs * ls
        self.t = self.tp + self.ts
        self.groups = 1 + bs
        assert self.tp % ls == 0 or self.tp % lp == 0

    def row_tile(self, cap=1024):
        return _pick(math.gcd(self.tp, self.ls), [c for c in (1024, 512, 256, 128, 64, 32, 16, 8) if c <= cap])

    def group_of(self, row_start):
        return jnp.where(row_start < self.tp, 0, 1 + (row_start - self.tp) // self.ls)


def _mm_body(*refs, pre, swiglu, epi, nk, nj, grouped, slab_in, slab_out, precise, tn):
    refs = list(refs)
    tile_row_ref = None
    if grouped:
        refs.pop(0)
        tile_row_ref = refs.pop(0)
    x_ref = refs.pop(0)
    g_ref = refs.pop(0) if pre else None
    sh_ref = sc_ref = None
    if pre == "normmod":
        sh_ref = refs.pop(0)
        sc_ref = refs.pop(0)
    w_ref = refs.pop(0)
    w2_ref = refs.pop(0) if swiglu else None
    res_ref = gate_ref = None
    if epi == "resgate":
        res_ref = refs.pop(0)
        gate_ref = refs.pop(0)
    o_ref = refs.pop(0)
    h_scr = refs.pop(0) if (pre or slab_in) else None
    acc_ref = refs.pop(0) if nk > 1 else None
    i = pl.program_id(0)
    j = pl.program_id(1)
    k = pl.program_id(2)

    def prologue():
        if slab_in:
            tm, kdim = h_scr.shape
            slab = kdim // LANES
            for s in range(slab):
                h_scr[:, s * LANES:(s + 1) * LANES] = x_ref[pl.ds(s, tm, stride=slab), :].astype(BF16)
        else:
            xf = x_ref[...].astype(F32)
            ms = jnp.mean(xf * xf, axis=-1, keepdims=True)
            y = (xf * lax.rsqrt(ms + EPS)) * g_ref[...]
            if pre == "normmod":
                y = y * (1.0 + sc_ref[0]) + sh_ref[0]
            h_scr[...] = y.astype(BF16)

    def product(wr):
        if precise:
            xf, wf = x_ref[...], wr[...]
            xh, wh = xf.astype(BF16), wf.astype(BF16)
            xl = (xf - xh.astype(F32)).astype(BF16)
            wl = (wf - wh.astype(F32)).astype(BF16)
            return _dot(xh, wh) + (_dot(xh, wl) + _dot(xl, wh))
        if pre or slab_in:
            xb = h_scr[...]
        elif x_ref.shape[1] != wr.shape[0]:
            tk = wr.shape[0]
            xb = x_ref[:, pl.ds(pl.multiple_of(k * tk, tk), tk)].astype(BF16)
        else:
            xb = x_ref[...].astype(BF16)
        return _dot(xb, wr[...].astype(BF16))

    def store_slab(a, s0):
        slab = (nj * tn) // LANES
        tm = o_ref.shape[0] // slab
        for q in range(tn // LANES):
            o_ref[pl.ds(s0 + q, tm, stride=slab), :] = a[:, q * LANES:(q + 1) * LANES].astype(o_ref.dtype)

    def finish(a, a2):
        if swiglu:
            a = _silu(a) * a2
        if epi == "resgate":
            a = res_ref[...] + gate_ref[0] * a
        if slab_out:
            for jj in range(nj):
                pl.when(j == jj)(functools.partial(store_slab, a, jj * (tn // LANES)))
        else:
            o_ref[...] = a.astype(o_ref.dtype)

    def compute():
        if pre or slab_in:
            pl.when(j == 0)(prologue)
        if nk > 1:
            @pl.when(k == 0)
            def _():
                acc_ref[...] = jnp.zeros_like(acc_ref)
        a = product(w_ref)
        a2 = product(w2_ref) if swiglu else None
        if nk == 1:
            finish(a, a2)
        else:
            acc_ref[...] += a

            @pl.when(k == nk - 1)
            def _():
                finish(acc_ref[...], None)

    if grouped:
        used = tile_row_ref[i] == i
        pl.when(used)(compute)

        @pl.when(jnp.logical_not(used))
        def _():
            o_ref[...] = jnp.zeros_like(o_ref)
    else:
        compute()


def _matmul(x, w, *, tm, tn, tk=None, w2=None, pre=None, g=None, mods=None, mod_base=None, tok=None,
            shift_idx=None, scale_idx=None, epi=None, res=None, gate_idx=None, out_dtype=F32,
            x_col0=0, n_rows=None, row0=0, w_col0=0, n_cols=None, w_lead=(), tiles=None, slab_in=False,
            slab_out=False, x_resident=False, precise=False, name="mm"):
    grouped = tiles is not None
    kdim = w.shape[-2]
    n = (w.shape[-1] - w_col0) if n_cols is None else n_cols
    tk = kdim if tk is None else tk
    nk = kdim // tk
    assert kdim % tk == 0 and n % tn == 0 and w_col0 % tn == 0
    wcb0 = w_col0 // tn
    swiglu = w2 is not None
    assert nk == 1 or not (pre or swiglu)
    x_resident = x_resident and nk > 1
    if slab_in:
        assert x.shape[1] == LANES and kdim % LANES == 0 and nk == 1 and not pre
        rows = x.shape[0] // (kdim // LANES) if n_rows is None else n_rows
    else:
        rows = (x.shape[0] - row0) if n_rows is None else n_rows
    assert rows % tm == 0 and row0 % tm == 0
    ni, nj = (tiles[0].shape[0] if grouped else rows // tm), n // tn
    rb0 = row0 // tm
    xcb0 = x_col0 // tk
    assert x_col0 % tk == 0

    def rowblk(i, pref):
        return pref[1][i] if grouped else i + rb0

    def colblk(i, j, pref):
        return jnp.where(pref[1][i] == i, j, nj - 1) if grouped else j

    def kblk(i, k, pref):
        return jnp.where(pref[1][i] == i, k, nk - 1) if grouped else k

    def grp(i):
        return tok.group_of(i * tm)

    in_specs, args = [], []
    once = dict(pipeline_mode=pl.Buffered(1)) if x_resident else {}
    if slab_in:
        in_specs.append(pl.BlockSpec((tm * (kdim // LANES), LANES), lambda i, j, k, *p: (rowblk(i, p), 0)))
    elif x_resident:
        assert x_col0 == 0 and x.shape[1] == kdim and nk > 1
        in_specs.append(pl.BlockSpec((tm, kdim), lambda i, j, k, *p: (rowblk(i, p), 0), **once))
    else:
        in_specs.append(pl.BlockSpec((tm, tk), lambda i, j, k, *p: (rowblk(i, p), xcb0 + kblk(i, k, p)), **once))
    args.append(x)
    if pre:
        in_specs.append(pl.BlockSpec((1, tk), lambda i, j, k, *p: (0, 0)))
        args.append(g.reshape(1, tk).astype(F32))
    if pre == "normmod":
        for idx in (shift_idx, scale_idx):
            in_specs.append(pl.BlockSpec((1, 1, tk), lambda i, j, k, *p, idx=idx: (mod_base + grp(i) * 6 + idx, 0, 0)))
            args.append(mods)
    lead = tuple(w_lead)
    none = (None,) * len(lead)
    wspec = (pl.BlockSpec(none + (None, tk, tn),
                          lambda i, j, k, *p: lead + (p[0][i], kblk(i, k, p), wcb0 + colblk(i, j, p))) if grouped
             else pl.BlockSpec(none + (tk, tn), lambda i, j, k, *p: lead + (k, wcb0 + j)))
    in_specs.append(wspec)
    args.append(w)
    if swiglu:
        in_specs.append(wspec)
        args.append(w2)
    if epi == "resgate":
        in_specs.append(pl.BlockSpec((tm, tn), lambda i, j, k, *p: (i + rb0, j)))
        args.append(res)
        in_specs.append(pl.BlockSpec((1, 1, tn), lambda i, j, k, *p: (mod_base + grp(i) * 6 + gate_idx, 0, j)))
        args.append(mods)
    if slab_out:
        assert tn % LANES == 0
        out_shape = jax.ShapeDtypeStruct((rows * (n // LANES), LANES), out_dtype)
        out_spec = pl.BlockSpec((tm * (n // LANES), LANES), lambda i, j, k, *p: (i, 0))
    else:
        out_shape = jax.ShapeDtypeStruct((rows, n), out_dtype)
        out_spec = pl.BlockSpec((tm, tn), lambda i, j, k, *p: (i, j))
    scratch = []
    if pre or slab_in:
        scratch.append(pltpu.VMEM((tm, tk), BF16))
    if nk > 1:
        scratch.append(pltpu.VMEM((tm, tn), F32))
    body = functools.partial(_mm_body, pre=pre, swiglu=swiglu, epi=epi, nk=nk, nj=nj, grouped=grouped,
                             slab_in=slab_in, slab_out=slab_out, precise=precise, tn=tn)
    grid_spec = pltpu.PrefetchScalarGridSpec(
        num_scalar_prefetch=2 if grouped else 0, grid=(ni, nj, nk),
        in_specs=in_specs, out_specs=out_spec, scratch_shapes=scratch)
    call = pl.pallas_call(body, grid_spec=grid_spec, out_shape=out_shape,
                          compiler_params=_cparams(3), name=name)
    return call(*tiles, *args) if grouped else call(*args)


def _mod_body(c_ref, w_ref, b_ref, o_ref):
    cb = _silu(c_ref[...]).astype(BF16)
    o_ref[...] = _dot(cb, w_ref[...].astype(BF16)) + b_ref[...]


def _modulation(cond, w_mod, b_mod):
    depth, d, n = w_mod.shape
    gp = cond.shape[0]
    tn = _pick(n, (1024, 512, 256, 128))
    return pl.pallas_call(
        _mod_body,
        grid=(depth, n // tn),
        in_specs=[pl.BlockSpec((gp, d), lambda i, j: (0, 0)),
                  pl.BlockSpec((None, d, tn), lambda i, j: (i, 0, j)),
                  pl.BlockSpec((None, 1, tn), lambda i, j: (i, 0, j))],
        out_specs=pl.BlockSpec((None, gp, tn), lambda i, j: (i, 0, j)),
        out_shape=jax.ShapeDtypeStruct((depth, gp, n), F32),
        compiler_params=_cparams(2), name="modulation",
    )(cond, w_mod, b_mod.reshape(depth, 1, n))


def _rms_body(x_ref, g_ref, o_ref):
    xf = x_ref[...]
    ms = jnp.mean(xf * xf, axis=-1, keepdims=True)
    o_ref[...] = (xf * lax.rsqrt(ms + EPS)) * g_ref[...]


def _rmsnorm(x, g, tm):
    t, d = x.shape
    return pl.pallas_call(
        _rms_body, grid=(t // tm,),
        in_specs=[pl.BlockSpec((tm, d), lambda i: (i, 0)), pl.BlockSpec((1, d), lambda i: (0, 0))],
        out_specs=pl.BlockSpec((tm, d), lambda i: (i, 0)),
        out_shape=jax.ShapeDtypeStruct((t, d), F32),
        compiler_params=_cparams(1), name="final_norm",
    )(x, g.reshape(1, d))


def _dft_tables(l):
    f = jnp.arange(l, dtype=jnp.int32)
    m = (f[:, None] * f[None, :]) % (2 * l)
    ang = m.astype(F32) * (math.pi / l)
    return jnp.cos(ang), jnp.sin(ang)


def _hy_features(l):
    t = jnp.arange(l, dtype=F32)
    t01 = t / (l - 1)
    bands = jnp.linspace(1e-4, HY_BANDS - 1, HY_BANDS, dtype=F32)
    ang = (2.0 * math.pi / l) * t[:, None] * bands[None, :]
    z = jnp.concatenate([t01[:, None], jnp.cos(ang), jnp.sin(ang)], axis=-1)
    return jnp.pad(z, ((0, 0), (0, 128 - z.shape[1])))


def _hy_taps_body(z_ref, w1_ref, b1_ref, w2_ref, b2_ref, w3f_ref, w3b_ref, dec_ref, ksum_ref, kdif_ref, knyq_ref):
    l = z_ref.shape[0]
    dot = functools.partial(jnp.dot, preferred_element_type=F32, precision=HIGHEST)
    f = jnp.sin(HY_SIN_FREQ * (dot(z_ref[...], w1_ref[...]) + b1_ref[...]))
    f = jnp.sin(HY_SIN_FREQ * (dot(f, w2_ref[...]) + b2_ref[...]))
    row = lax.broadcasted_iota(jnp.int32, (l, 1), 0)
    t01 = row.astype(F32) / (l - 1)
    kf = dot(f, w3f_ref[...]) * jnp.exp(-t01 * dec_ref[0:1, :])
    kb = dot(f, w3b_ref[...]) * jnp.exp(-t01 * dec_ref[1:2, :])
    kb = jnp.where(row == 0, 0.0, kb)
    l1 = jnp.sum(jnp.abs(kf), axis=0, keepdims=True) + jnp.sum(jnp.abs(kb), axis=0, keepdims=True)
    kf = kf / l1
    kb = kb / l1
    ks = kf + kb
    alt = jnp.where(row % 2 == 0, 1.0, -1.0)
    ksum_ref[...] = ks
    kdif_ref[...] = kb - kf
    knyq_ref[...] = jnp.sum(ks * alt, axis=0, keepdims=True)


def _hy_taps(l, w1, b1, w2, b2, w3, decay):
    d = decay.shape[1]
    nf = w2.shape[0]
    cb = _pick(d, (512, 256, 128))
    z = _hy_features(l)
    w1p = jnp.pad(w1, ((0, 128 - w1.shape[0]), (0, 0)))
    full = lambda shape: pl.BlockSpec(shape, lambda c: (0,) * len(shape))
    return pl.pallas_call(
        _hy_taps_body, grid=(d // cb,),
        in_specs=[full((l, 128)), full((128, nf)), full((1, nf)), full((nf, nf)), full((1, nf)),
                  pl.BlockSpec((nf, cb), lambda c: (0, c)),
                  pl.BlockSpec((nf, cb), lambda c: (0, d // cb + c)),
                  pl.BlockSpec((2, cb), lambda c: (0, c))],
        out_specs=[pl.BlockSpec((l, cb), lambda c: (0, c)), pl.BlockSpec((l, cb), lambda c: (0, c)),
                   pl.BlockSpec((1, cb), lambda c: (0, c))],
        out_shape=[jax.ShapeDtypeStruct((l, d), F32), jax.ShapeDtypeStruct((l, d), F32),
                   jax.ShapeDtypeStruct((1, d), F32)],
        compiler_params=_cparams(1), name="hyena_taps",
    )(z, w1p, b1.reshape(1, nf), w2, b2.reshape(1, nf), w3, w3, decay)


def _dwconv3(a, w_ref, b_ref, row, l):
    prev = jnp.where(row == 0, 0.0, pltpu.roll(a, 1, axis=0))
    nxt = jnp.where(row == l - 1, 0.0, pltpu.roll(a, l - 1, axis=0))
    return prev * w_ref[0:1, :] + a * w_ref[1:2, :] + nxt * w_ref[2:3, :] + b_ref[...]


def _hy_core_body(x0_ref, x1_ref, v_ref, w0_ref, w1_ref, wv_ref, b0_ref, b1_ref, bv_ref, skip_ref,
                  kr_ref, ki_ref, knyq_ref, c_ref, s_ref, o_ref):
    l = x0_ref.shape[0]
    row = lax.broadcasted_iota(jnp.int32, (l, 1), 0)
    x0 = _dwconv3(x0_ref[...], w0_ref, b0_ref, row, l)
    x1 = _dwconv3(x1_ref[...], w1_ref, b1_ref, row, l)
    v = _dwconv3(v_ref[...], wv_ref, bv_ref, row, l)
    z = v * x1
    zb = z.astype(BF16)
    cm = c_ref[...]
    sm = s_ref[...]
    zr = _dot(cm, zb)
    zs = _dot(sm, zb)
    wf = jnp.where(row == 0, 0.5 / l, 1.0 / l)
    kr = kr_ref[...] * wf
    ki = ki_ref[...] * wf
    a = (zr * kr + zs * ki).astype(BF16)
    b = (zs * kr - zr * ki).astype(BF16)
    alt = jnp.where(row % 2 == 0, 1.0, -1.0)
    znyq = jnp.sum(z * alt, axis=0, keepdims=True)
    y = _dot(cm, a) + _dot(sm, b) + alt * (znyq * knyq_ref[...] * (0.5 / l))
    o_ref[...] = (x0 * (y + z * skip_ref[...])).astype(o_ref.dtype)


def _hy_core(u, row0, nseq, l, d, conv_w, conv_b, skip, kr, ki, knyq, cmat, smat):
    cb = _pick(d, (128,) if l > 512 else (512, 256, 128))
    nd = d // cb
    sb0 = row0 // l
    assert row0 % l == 0
    useq = lambda sec: pl.BlockSpec((l, cb), lambda s, c, sec=sec: (sb0 + s, sec * nd + c))
    wsec = lambda sec: pl.BlockSpec((3, cb), lambda s, c, sec=sec: (0, sec * nd + c))
    bsec = lambda sec: pl.BlockSpec((1, cb), lambda s, c, sec=sec: (0, sec * nd + c))
    col = lambda rows: pl.BlockSpec((rows, cb), lambda s, c: (0, c))
    const = pl.BlockSpec((l, l), lambda s, c: (0, 0), pipeline_mode=pl.Buffered(1))
    cb3 = conv_b.reshape(1, 3 * d)
    return pl.pallas_call(
        _hy_core_body, grid=(nseq, nd),
        in_specs=[useq(0), useq(1), useq(2), wsec(0), wsec(1), wsec(2), bsec(0), bsec(1), bsec(2),
                  col(1), col(l), col(l), col(1), const, const],
        out_specs=pl.BlockSpec((l, cb), lambda s, c: (s, c)),
        out_shape=jax.ShapeDtypeStruct((nseq * l, d), BF16),
        compiler_params=_cparams(2), name=f"hyena_core_{l}",
    )(u, u, u, conv_w, conv_w, conv_w, cb3, cb3, cb3, skip.reshape(1, d), kr, ki, knyq, cmat, smat)


def _hyena(tok, x, mods, mod_base, g, lyr, w_in, conv_w, conv_b, f_w1, f_b1, f_w2, f_b2, f_w3, decay, skip, w_out):
    d = x.shape[1]
    tm = tok.row_tile()
    u = _matmul(x, w_in, w_lead=(lyr,), tm=tm, tn=_pick(3 * d, (512, 256, 128)), pre="normmod", g=g, mods=mods,
                mod_base=mod_base, tok=tok, shift_idx=0, scale_idx=1, name="hyena_in")
    parts = []
    for row0, nseq, l in ((0, tok.bp, tok.lp), (tok.tp, tok.bs, tok.ls)):
        ksum, kdif, knyq = _hy_taps(l, f_w1, f_b1, f_w2, f_b2, f_w3, decay)
        cmat, smat = _dft_tables(l)
        tf = _pick(l, (512, 256, 128))
        tn = _pick(d, (512, 256, 128))
        kr = _matmul(cmat, ksum, tm=tf, tn=tn, precise=True, name="hyena_spec_re")
        ki = _matmul(smat, kdif, tm=tf, tn=tn, precise=True, name="hyena_spec_im")
        parts.append(_hy_core(u, row0, nseq, l, d, conv_w, conv_b, skip, kr, ki, knyq,
                              cmat.astype(BF16), smat.astype(BF16)))
    gmix = jnp.concatenate(parts, axis=0)
    return _matmul(gmix, w_out, w_lead=(lyr,), tm=tm, tn=_pick(d, (512, 256, 128)), epi="resgate", res=x,
                   mods=mods, mod_base=mod_base, tok=tok, gate_idx=2, name="hyena_out")


def _ssd_conv_body(x_ref, w_ref, b_ref, o_ref):
    l = x_ref.shape[0]
    row = lax.broadcasted_iota(jnp.int32, (l, 1), 0)
    o_ref[...] = _silu(_dwconv3(x_ref[...], w_ref, b_ref, row, l))


def _ssd_conv(proj, col0, conv_w, conv_b, row0, nseq, l):
    cdim = conv_w.shape[1]
    cb = _pick(math.gcd(cdim, col0), (512, 256, 128))
    cb0, sb0 = col0 // cb, row0 // l
    return pl.pallas_call(
        _ssd_conv_body, grid=(nseq, cdim // cb),
        in_specs=[pl.BlockSpec((l, cb), lambda s, c: (sb0 + s, cb0 + c)),
                  pl.BlockSpec((3, cb), lambda s, c: (0, c)),
                  pl.BlockSpec((1, cb), lambda s, c: (0, c))],
        out_specs=pl.BlockSpec((l, cb), lambda s, c: (s, c)),
        out_shape=jax.ShapeDtypeStruct((nseq * l, cdim), F32),
        compiler_params=_cparams(2), name=f"ssd_conv_{l}",
    )(proj, conv_w, conv_b.reshape(1, cdim))


def _ssd_dt_body(raw_ref, bias_ref, alog_ref, dt_ref, cum_ref):
    tm, w = raw_ref.shape
    xr = raw_ref[...] + bias_ref[...]
    dt = jnp.maximum(xr, 0.0) + jnp.log1p(jnp.exp(-jnp.abs(xr)))
    dt_ref[...] = dt
    a = dt * (-jnp.exp(alog_ref[...]))
    ri = lax.broadcasted_iota(jnp.int32, (SSD_CHUNK, SSD_CHUNK), 0)
    ci = lax.broadcasted_iota(jnp.int32, (SSD_CHUNK, SSD_CHUNK), 1)
    lower = (ci <= ri).astype(F32)
    upper = (ci >= ri).astype(F32)
    lane = lax.broadcasted_iota(jnp.int32, (SSD_CHUNK, w), 1)
    for q in range(tm // SSD_CHUNK):
        ac = a[q * SSD_CHUNK:(q + 1) * SSD_CHUNK, :]
        pc = jnp.dot(lower, ac, preferred_element_type=F32, precision=HIGHEST)
        rc = jnp.dot(upper, ac, preferred_element_type=F32, precision=HIGHEST)
        cum_ref[q * SSD_CHUNK:(q + 1) * SSD_CHUNK, :] = jnp.where(lane < w // 2, pc, rc)


def _ssd_dt(proj, col0, dt_bias, a_log, tm):
    t = proj.shape[0]
    w = dt_bias.size
    assert col0 % w == 0
    return pl.pallas_call(
        _ssd_dt_body, grid=(t // tm,),
        in_specs=[pl.BlockSpec((tm, w), lambda i: (i, col0 // w)),
                  pl.BlockSpec((1, w), lambda i: (0, 0)), pl.BlockSpec((1, w), lambda i: (0, 0))],
        out_specs=[pl.BlockSpec((tm, w), lambda i: (i, 0)), pl.BlockSpec((tm, w), lambda i: (i, 0))],
        out_shape=[jax.ShapeDtypeStruct((t, w), F32), jax.ShapeDtypeStruct((t, w), F32)],
        compiler_params=_cparams(1), name="ssd_dt",
    )(proj, dt_bias.reshape(1, w), a_log.reshape(1, w))


def _ssd_scan_body(*refs, has_init, r, p, nc):
    refs = list(refs)
    x_ref, b_ref, c_ref, dsk_ref = refs[:4]
    dt_refs, cum_refs, cumt_refs = refs[4:6], refs[6:8], refs[8:10]
    refs = refs[10:]
    init_refs = (refs.pop(0), refs.pop(0)) if has_init else None
    y_ref, fin_refs, st_ref = refs[0], refs[1:3], refs[3]
    ch = SSD_CHUNK
    npair = r // 2
    lane = lax.broadcasted_iota(jnp.int32, (ch, 2 * p), 1)
    left = lane < p
    ri = lax.broadcasted_iota(jnp.int32, (ch, ch), 0)
    ci = lax.broadcasted_iota(jnp.int32, (ch, ch), 1)

    for d in range(2):
        keep = (ci <= ri) if d == 0 else (ci >= ri)
        for pr in range(npair):
            if has_init:
                st_ref[pr] = init_refs[d][pr * 2 * p:(pr + 1) * 2 * p, :].T
            else:
                st_ref[pr] = jnp.zeros((st_ref.shape[1], 2 * p), F32)

        def chunk(ic, carry, d=d, keep=keep):
            cidx = ic if d == 0 else nc - 1 - ic
            r0 = pl.multiple_of(cidx * ch, ch)
            rows = pl.ds(r0, ch)
            bc = b_ref[rows, :].astype(BF16)
            cc = c_ref[rows, :].astype(BF16)
            cbm = _dot_nt(cc, bc)
            bt = b_ref[rows, :].T.astype(BF16)
            dtc = dt_refs[d][rows, :]
            cumc = cum_refs[d][rows, :]
            cumt = cumt_refs[d][:, rows]
            edge = cumc[0:1, :] if d == 1 else cumc[ch - 1:ch, :]
            for pr in range(npair):
                xpair = x_ref[rows, pr * 2 * p:(pr + 1) * 2 * p]
                ha, hb = 2 * pr, 2 * pr + 1
                sel = lambda col: jnp.where(left, col[:, ha:ha + 1], col[:, hb:hb + 1])
                xs = xpair * sel(dtc)
                xsb = xs.astype(BF16)
                ydiag = []
                for h in (ha, hb):
                    seg = cumc[:, h:h + 1] - cumt[h:h + 1, :]
                    dec = jnp.exp(jnp.where(keep, seg, -jnp.inf))
                    ydiag.append(_dot((cbm * dec).astype(BF16), xsb))
                cum_pair = sel(cumc)
                st = st_ref[pr]
                yoff = _dot(cc, st.astype(BF16)) * jnp.exp(cum_pair)
                yc = jnp.where(left, ydiag[0], ydiag[1]) + yoff
                edge_pair = jnp.where(left[0:1, :], edge[:, ha:ha + 1], edge[:, hb:hb + 1])
                xsd = (xs * jnp.exp(edge_pair - cum_pair)).astype(BF16)
                st_ref[pr] = st * jnp.exp(edge_pair) + _dot(bt, xsd)
                cols = slice(pr * 2 * p, (pr + 1) * 2 * p)
                if d == 0:
                    y_ref[rows, cols] = yc + xpair * dsk_ref[:, cols]
                else:
                    y_ref[rows, cols] += yc
            return carry

        lax.fori_loop(0, nc, chunk, 0)
        for pr in range(npair):
            fin_refs[d][pr * 2 * p:(pr + 1) * 2 * p, :] = st_ref[pr].T


def _ssd_scan(xbc, dtg, cumg, cumtg, d_vec, nseq, l, di, g, n, p, init):
    h = di // p
    r = h // g
    rp = r * p
    assert r % 2 == 0 and rp % 128 == 0 and n % 128 == 0 and l % SSD_CHUNK == 0
    nb0, nc0 = di // n, (di + g * n) // n
    has_init = init is not None
    dspec = lambda d: pl.BlockSpec((None, l, r), lambda s, gi, d=d: (d * g + gi, s, 0))
    tspec = lambda d: pl.BlockSpec((None, r, l), lambda s, gi, d=d: (d * g + gi, 0, s))
    sspec = pl.BlockSpec((None, rp, n), lambda s, gi: (s, gi, 0))
    in_specs = [pl.BlockSpec((l, rp), lambda s, gi: (s, gi)),
                pl.BlockSpec((l, n), lambda s, gi: (s, nb0 + gi)),
                pl.BlockSpec((l, n), lambda s, gi: (s, nc0 + gi)),
                pl.BlockSpec((1, rp), lambda s, gi: (0, gi)),
                dspec(0), dspec(1), dspec(0), dspec(1), tspec(0), tspec(1)]
    args = [xbc, xbc, xbc, d_vec, dtg, dtg, cumg, cumg, cumtg, cumtg]
    if has_init:
        in_specs += [sspec, sspec]
        args += [s.reshape(nseq, h * p, n) for s in init]
    st_shape = jax.ShapeDtypeStruct((nseq, h * p, n), F32)
    body = functools.partial(_ssd_scan_body, has_init=has_init, r=r, p=p, nc=l // SSD_CHUNK)
    y, sf, sb = pl.pallas_call(
        body, grid=(nseq, g), in_specs=in_specs,
        out_specs=[pl.BlockSpec((l, rp), lambda s, gi: (s, gi)), sspec, sspec],
        out_shape=[jax.ShapeDtypeStruct((nseq * l, di), F32), st_shape, st_shape],
        scratch_shapes=[pltpu.VMEM((r // 2, n, 2 * p), F32)],
        compiler_params=_cparams(2), name=f"ssd_scan_{l}",
    )(*args)
    return y, sf.reshape(nseq, h, p, n), sb.reshape(nseq, h, p, n)


def _ssd_gate_body(y_ref, z_ref, g_ref, o_ref):
    v = y_ref[...] * _silu(z_ref[...])
    ms = jnp.mean(v * v, axis=-1, keepdims=True)
    o_ref[...] = ((v * lax.rsqrt(ms + EPS)) * g_ref[...]).astype(o_ref.dtype)


def _ssd_gate(y, proj, norm_g, tm):
    t, di = y.shape
    return pl.pallas_call(
        _ssd_gate_body, grid=(t // tm,),
        in_specs=[pl.BlockSpec((tm, di), lambda i: (i, 0)), pl.BlockSpec((tm, di), lambda i: (i, 0)),
                  pl.BlockSpec((1, di), lambda i: (0, 0))],
        out_specs=pl.BlockSpec((tm, di), lambda i: (i, 0)),
        out_shape=jax.ShapeDtypeStruct((t, di), BF16),
        compiler_params=_cparams(1), name="ssd_gate",
    )(y, proj, norm_g.reshape(1, di))


def _ssd(tok, x, mods, mod_base, gnorm, w_in, conv_w, conv_b, dt_bias, a_log, d_skip, norm_g, w_out,
         init_f, init_b):
    d = x.shape[1]
    p, n = init_f.shape[-2], init_f.shape[-1]
    h = dt_bias.shape[1]
    di = h * p
    cdim = conv_w.shape[1]
    g = (cdim - di) // (2 * n)
    r = h // g
    tm = tok.row_tile()
    pre_kw = dict(pre="normmod", g=gnorm, mods=mods, mod_base=mod_base, tok=tok, shift_idx=0, scale_idx=1)
    proj = _matmul(x, w_in, tm=tm, tn=_pick(math.gcd(di, cdim), (512, 256, 128)), n_cols=di + cdim,
                   name="ssd_in", **pre_kw)
    dt_raw = _matmul(x, w_in, tm=tm, tn=2 * h, w_col0=di + cdim, name="ssd_in_dt", **pre_kw)
    dt, cum = _ssd_dt(dt_raw, 0, dt_bias, a_log, _pick(tok.t, (512, 256, 128)))
    split = lambda a: jnp.transpose(a.reshape(tok.t, 2 * g, r), (1, 0, 2))
    dtg, cumg = split(dt), split(cum)
    cumtg = jnp.transpose(cumg, (0, 2, 1))
    d_vec = jnp.repeat(d_skip.astype(F32), p).reshape(1, di)
    ys, states = [], []
    for row0, nseq, l, init in ((0, tok.bp, tok.lp, None), (tok.tp, tok.bs, tok.ls, (init_f, init_b))):
        xbc = _ssd_conv(proj, di, conv_w, conv_b, row0, nseq, l)
        sl = slice(row0, row0 + nseq * l)
        y, sf, sb = _ssd_scan(xbc, dtg[:, sl], cumg[:, sl], cumtg[:, :, sl], d_vec, nseq, l, di, g, n, p, init)
        ys.append(y)
        states.append((sf, sb))
    yg = _ssd_gate(jnp.concatenate(ys, axis=0), proj, norm_g, _pick(tok.t, (256, 128)))
    out = _matmul(yg, w_out, tm=tm, tn=_pick(d, (512, 256, 128)), tk=_pick(di, (2048, 1024, 512, 256)),
                  epi="resgate", res=x, mods=mods, mod_base=mod_base, tok=tok, gate_idx=2, name="ssd_out")
    return out, states[0]


def _rope_tables(n_tokens, dr):
    axis_dim = dr // 2
    rows = n_tokens // GRID_W
    row = jnp.broadcast_to(jnp.arange(rows, dtype=F32)[:, None], (rows, GRID_W)).reshape(-1)
    col = jnp.broadcast_to(jnp.arange(GRID_W, dtype=F32)[None, :], (rows, GRID_W)).reshape(-1)
    inv_freq = ROPE_BASE ** (-jnp.arange(0, axis_dim, 2, dtype=F32) / axis_dim)
    ang = jnp.concatenate([row[:, None] * inv_freq, col[:, None] * inv_freq], axis=-1)
    return jnp.cos(ang), jnp.sin(ang)


def _rot_cols(w, dr):
    half = dr // 2
    return jnp.concatenate([-w[..., half:], w[..., :half]], axis=-1)


def _mla_kv_body(kv_ref, pe_ref, g_ref, cs_ref, ckv_ref, kpe_ref):
    kv = kv_ref[...]
    ms = jnp.mean(kv * kv, axis=-1, keepdims=True)
    ckv_ref[...] = (kv * lax.rsqrt(ms + EPS)) * g_ref[...]
    prod = pe_ref[...] * cs_ref[...]
    kpe_ref[...] = prod + pltpu.roll(prod, prod.shape[1] // 2, axis=1)


def _mla_kv(qkv, col0, rank, dr, g, cs, tm):
    t = qkv.shape[0]
    assert col0 % rank == 0 and (col0 + rank) % (2 * dr) == 0 and 2 * dr == 128
    return pl.pallas_call(
        _mla_kv_body, grid=(t // tm,),
        in_specs=[pl.BlockSpec((tm, rank), lambda i: (i, col0 // rank)),
                  pl.BlockSpec((tm, 2 * dr), lambda i: (i, (col0 + rank) // (2 * dr))),
                  pl.BlockSpec((1, rank), lambda i: (0, 0)),
                  pl.BlockSpec((tm, 2 * dr), lambda i: (i, 0))],
        out_specs=[pl.BlockSpec((tm, rank), lambda i: (i, 0)), pl.BlockSpec((tm, 2 * dr), lambda i: (i, 0))],
        out_shape=[jax.ShapeDtypeStruct((t, rank), F32), jax.ShapeDtypeStruct((t, 2 * dr), F32)],
        compiler_params=_cparams(1), name="mla_kv",
    )(qkv, qkv, g.reshape(1, rank), cs)


def _attn_body(qn_ref, qp_ref, qr_ref, cos_ref, sin_ref, kv_ref, kpe_ref, o_ref, *, dn, dr, dv, scale):
    qrot = qp_ref[...] * cos_ref[...] + qr_ref[...] * sin_ref[...]
    lane = lax.broadcasted_iota(jnp.int32, qrot.shape, 1)
    kpe = kpe_ref[...].astype(BF16)
    for h in range(2):
        qn = qn_ref[:, h * dn:(h + 1) * dn].astype(BF16)
        kn = kv_ref[:, h * (dn + dv):h * (dn + dv) + dn]
        v = kv_ref[:, h * (dn + dv) + dn:(h + 1) * (dn + dv)]
        qh = jnp.where((lane >= h * dr) & (lane < (h + 1) * dr), qrot, 0.0).astype(BF16)
        s = _dot_nt(jnp.concatenate([qn, qh], axis=1), jnp.concatenate([kn, kpe], axis=1))
        m = jnp.max(s, axis=-1, keepdims=True)
        e = jnp.exp((s - m) * scale)
        o = _dot(e.astype(BF16), v) / jnp.sum(e, axis=-1, keepdims=True)
        o_ref[:, h * dv:(h + 1) * dv] = o.astype(o_ref.dtype)


def _attention(q, cosq, sinq, kv, kpe, row0, nb, lq, lk, heads, dn, dr, dv):
    assert dn == 128 and dv == 128 and dr == 64 and heads % 2 == 0
    tq = _pick(lq, (512, 256, 128))
    nq = lq // tq
    rb0 = row0 // tq
    hp = heads // 2
    pe0, rot0 = heads * dn // 128, (heads * dn + heads * dr) // 128
    body = functools.partial(_attn_body, dn=dn, dr=dr, dv=dv, scale=(dn + dr) ** -0.5)
    return pl.pallas_call(
        body, grid=(nb, hp, nq),
        in_specs=[pl.BlockSpec((tq, 2 * dn), lambda b, h, i: (rb0 + b * nq + i, h)),
                  pl.BlockSpec((tq, 128), lambda b, h, i: (rb0 + b * nq + i, pe0 + h)),
                  pl.BlockSpec((tq, 128), lambda b, h, i: (rb0 + b * nq + i, rot0 + h)),
                  pl.BlockSpec((tq, 128), lambda b, h, i: (rb0 + b * nq + i, 0)),
                  pl.BlockSpec((tq, 128), lambda b, h, i: (rb0 + b * nq + i, 0)),
                  pl.BlockSpec((None, lk, 2 * (dn + dv)), lambda b, h, i: (b, 0, h)),
                  pl.BlockSpec((None, lk, 128), lambda b, h, i: (b, 0, 0))],
        out_specs=pl.BlockSpec((tq, 2 * dv), lambda b, h, i: (b * nq + i, h)),
        out_shape=jax.ShapeDtypeStruct((nb * lq, heads * dv), BF16),
        compiler_params=_cparams(3), name=f"mla_attn_{lq}",
    )(q, q, q, cosq, sinq, kv, kpe)


def _mla(tok, x, mods, mod_base, gnorm, w_dq, q_norm_g, w_uq, w_dkv, kv_norm_g, w_ukv, w_o, cache_ckv, cache_kpe):
    d = x.shape[1]
    heads = MLA_HEADS
    qr_, kvr, dr = w_dq.shape[1], kv_norm_g.shape[0], cache_kpe.shape[-1]
    dv = w_o.shape[0] // heads
    dn = w_ukv.shape[1] // heads - dv
    past = cache_ckv.shape[1]
    tm = tok.row_tile()
    w_a = jnp.concatenate([w_dq, w_dkv, _rot_cols(w_dkv[:, kvr:], dr)], axis=1)
    wq = w_uq.reshape(qr_, heads, dn + dr)
    wq_pe = wq[:, :, dn:]
    w_q = jnp.concatenate([wq[:, :, :dn].reshape(qr_, heads * dn), wq_pe.reshape(qr_, heads * dr),
                           _rot_cols(wq_pe, dr).reshape(qr_, heads * dr)], axis=1)
    qkv = _matmul(x, w_a, tm=tm, tn=_pick(w_a.shape[1], (384, 256, 128)), pre="normmod", g=gnorm, mods=mods,
                  mod_base=mod_base, tok=tok, shift_idx=0, scale_idx=1, name="mla_down")
    cos, sin = _rope_tables(tok.ls, dr)
    cos2 = jnp.tile(jnp.concatenate([cos, cos], axis=1), (tok.bs, 1))
    sin2 = jnp.tile(jnp.concatenate([sin, sin], axis=1), (tok.bs, 1))
    ones, zeros = jnp.ones((tok.tp, dr), F32), jnp.zeros((tok.tp, dr), F32)
    cos_t = jnp.concatenate([ones, cos2], axis=0)
    sin_t = jnp.concatenate([zeros, sin2], axis=0)
    ckv, kpe = _mla_kv(qkv, qr_, kvr, dr, kv_norm_g, jnp.concatenate([cos_t, sin_t], axis=1), tm)
    q = _matmul(qkv, w_q, tm=tm, tn=_pick(w_q.shape[1], (512, 256, 128)), tk=qr_, pre="norm", g=q_norm_g,
                name="mla_q")
    cosq = jnp.concatenate([cos_t, cos_t], axis=1)
    sinq = jnp.concatenate([sin_t, sin_t], axis=1)
    ckv_s = jnp.concatenate([cache_ckv, ckv[tok.tp:].reshape(tok.bs, tok.ls, kvr)], axis=1)
    kpe_s = jnp.concatenate([jnp.tile(cache_kpe, (1, 1, 2)), kpe[tok.tp:].reshape(tok.bs, tok.ls, 2 * dr)], axis=1)
    lk_s = past + tok.ls
    ckv_all = jnp.concatenate([ckv[:tok.tp], ckv_s.reshape(tok.bs * lk_s, kvr)], axis=0)
    kv = _matmul(ckv_all, w_ukv, tm=_pick(math.gcd(tok.tp, tok.bs * lk_s), (1024, 512, 256, 128)),
                 tn=_pick(w_ukv.shape[1], (512, 256, 128)), out_dtype=BF16, name="mla_up_kv")
    nkv = kv.shape[1]
    o_p = _attention(q, cosq, sinq, kv[:tok.tp].reshape(tok.bp, tok.lp, nkv),
                     kpe[:tok.tp].reshape(tok.bp, tok.lp, 2 * dr), 0, tok.bp, tok.lp, tok.lp, heads, dn, dr, dv)
    o_s = _attention(q, cosq, sinq, kv[tok.tp:].reshape(tok.bs, lk_s, nkv), kpe_s, tok.tp, tok.bs, tok.ls, lk_s,
                     heads, dn, dr, dv)
    o = jnp.concatenate([o_p, o_s], axis=0)
    out = _matmul(o, w_o, tm=tm, tn=_pick(d, (512, 256, 128)), epi="resgate", res=x, mods=mods,
                  mod_base=mod_base, tok=tok, gate_idx=2, name="mla_out")
    new_ckv = ckv[:tok.tp].reshape(tok.bp, tok.lp, kvr)
    new_kpe = qkv[:tok.tp, qr_ + kvr:qr_ + kvr + dr].reshape(tok.bp, tok.lp, dr)
    return out, new_ckv, new_kpe


def _k_tile(kdim, cap):
    return max(c for c in range(LANES, cap + 1, LANES) if kdim % c == 0)


def _dense_ffn(tok, x, mods, mod_base, gnorm, lyr, w_gate, w_up, w_down):
    _, d, dff = w_gate.shape
    tm = tok.row_tile()
    hmid = _matmul(x, w_gate, w2=w_up, w_lead=(lyr,), tm=tm, tn=_pick(dff, (512, 256, 128)), pre="normmod", g=gnorm,
                   mods=mods, mod_base=mod_base, tok=tok, shift_idx=3, scale_idx=4, out_dtype=BF16, name="ffn_up")
    return _matmul(hmid, w_down, w_lead=(lyr,), tm=tm, tn=_pick(d, (512, 256, 128)), tk=_k_tile(dff, 3072),
                   x_resident=True, epi="resgate", res=x, mods=mods, mod_base=mod_base, tok=tok, gate_idx=5, name="ffn_down")


def _route_body(x_ref, g_ref, sh_ref, sc_ref, wr_ref, br_ref, h_ref, info_ref, cnt_ref, carry_ref, *, n_exp):
    i = pl.program_id(0)
    tm, d = x_ref.shape
    slab = d // LANES

    @pl.when(i == 0)
    def _():
        carry_ref[...] = jnp.zeros_like(carry_ref)

    xf = x_ref[...]
    ms = jnp.mean(xf * xf, axis=-1, keepdims=True)
    hh = ((xf * lax.rsqrt(ms + EPS)) * g_ref[...]) * (1.0 + sc_ref[0]) + sh_ref[0]
    for s in range(slab):
        h_ref[pl.ds(s, tm, stride=slab), :] = hh[:, s * LANES:(s + 1) * LANES]
    logits = jnp.dot(hh, wr_ref[...], preferred_element_type=F32, precision=HIGHEST) + br_ref[...]
    lane = lax.broadcasted_iota(jnp.int32, logits.shape, 1)
    logits = jnp.where(lane < n_exp, logits, -jnp.inf)
    t1 = jnp.max(logits, axis=-1, keepdims=True)
    i1 = jnp.min(jnp.where(logits == t1, lane, 128), axis=-1, keepdims=True)
    rest = jnp.where(lane == i1, -jnp.inf, logits)
    t2 = jnp.max(rest, axis=-1, keepdims=True)
    i2 = jnp.min(jnp.where(rest == t2, lane, 128), axis=-1, keepdims=True)
    e2 = jnp.exp(t2 - t1)
    g1 = 1.0 / (1.0 + e2)
    g2 = e2 / (1.0 + e2)
    oh1 = lane == i1
    oh2 = lane == i2
    cnt = jnp.where(oh1 | oh2, 1.0, 0.0).astype(BF16)
    ri = lax.broadcasted_iota(jnp.int32, (tm, tm), 0)
    ci = lax.broadcasted_iota(jnp.int32, (tm, tm), 1)
    before = _dot((ci < ri).astype(BF16), cnt) + carry_ref[...]
    r1 = jnp.sum(jnp.where(oh1, before, 0.0), axis=-1, keepdims=True)
    r2 = jnp.sum(jnp.where(oh2, before, 0.0), axis=-1, keepdims=True)
    carry_ref[...] += jnp.sum(cnt.astype(F32), axis=0, keepdims=True)
    vals = (i1.astype(F32), i2.astype(F32), r1, r2, g1, g2)
    info = jnp.zeros(logits.shape, F32)
    for q, v in enumerate(vals):
        info = jnp.where(lane == q, v, info)
    info_ref[...] = info
    cnt_ref[...] = carry_ref[...]


def _moe_route(tok, x, mods, mod_base, gnorm, w_router, b_router, tm):
    t, d = x.shape
    n_exp = w_router.shape[1]
    wr = jnp.pad(w_router, ((0, 0), (0, 128 - n_exp)))
    br = jnp.pad(b_router, (0, 128 - n_exp)).reshape(1, 128)
    slab = d // LANES
    modspec = lambda idx: pl.BlockSpec((1, 1, d), lambda i: (mod_base + tok.group_of(i * tm) * 6 + idx, 0, 0))
    return pl.pallas_call(
        functools.partial(_route_body, n_exp=n_exp), grid=(t // tm,),
        in_specs=[pl.BlockSpec((tm, d), lambda i: (i, 0)), pl.BlockSpec((1, d), lambda i: (0, 0)),
                  modspec(3), modspec(4),
                  pl.BlockSpec((d, 128), lambda i: (0, 0)), pl.BlockSpec((1, 128), lambda i: (0, 0))],
        out_specs=[pl.BlockSpec((tm * slab, LANES), lambda i: (i, 0)),
                   pl.BlockSpec((tm, 128), lambda i: (i, 0)),
                   pl.BlockSpec((1, 128), lambda i: (0, 0))],
        out_shape=[jax.ShapeDtypeStruct((t * slab, LANES), F32), jax.ShapeDtypeStruct((t, 128), F32),
                   jax.ShapeDtypeStruct((1, 128), F32)],
        scratch_shapes=[pltpu.VMEM((1, 128), F32)],
        compiler_params=_cparams(1), name="moe_route",
    )(x, gnorm.reshape(1, d), mods, mods, wr, br)


def _dispatch_body(dest_ref, h_ref, init_hbm, o_hbm, sem, *, tm, slab):
    del init_hbm
    base = pl.program_id(0) * tm

    def copies(r):
        src = h_ref.at[pl.ds(r * slab, slab)]
        return [pltpu.make_async_copy(src, o_hbm.at[pl.ds(dest_ref[2 * (base + r) + k] * slab, slab)], sem)
                for k in range(2)]

    def start(r, c):
        for k, cp in enumerate(copies(r)):
            cp.start(priority=k)
        return c

    def wait(r, c):
        for cp in copies(r):
            cp.wait()
        return c

    lax.fori_loop(0, tm, start, 0)
    lax.fori_loop(0, tm, wait, 0)


def _moe_dispatch(h_slab, dest, n_rows, slab, tm):
    t = h_slab.shape[0] // slab
    init = jnp.zeros((n_rows * slab, LANES), F32)
    grid_spec = pltpu.PrefetchScalarGridSpec(
        num_scalar_prefetch=1, grid=(t // tm,),
        in_specs=[pl.BlockSpec((tm * slab, LANES), lambda i, p: (i, 0)), pl.BlockSpec(memory_space=pl.ANY)],
        out_specs=pl.BlockSpec(memory_space=pl.ANY),
        scratch_shapes=[pltpu.SemaphoreType.DMA(())])
    return pl.pallas_call(
        functools.partial(_dispatch_body, tm=tm, slab=slab), grid_spec=grid_spec,
        out_shape=jax.ShapeDtypeStruct(init.shape, F32), input_output_aliases={2: 0},
        compiler_params=_cparams(1), name="moe_dispatch",
    )(dest, h_slab, init)


def _combine_body(pos_ref, x_ref, gate_ref, info_ref, y_hbm, o_ref, buf1, buf2, sem, *, tm):
    base = pl.program_id(0) * tm
    slab = buf1.shape[0] // tm

    def copies(r):
        rows = pl.ds(r * slab, slab)
        src = lambda k: y_hbm.at[pl.ds(pos_ref[2 * (base + r) + k] * slab, slab)]
        return (pltpu.make_async_copy(src(0), buf1.at[rows], sem),
                pltpu.make_async_copy(src(1), buf2.at[rows], sem))

    def start(r, c):
        for k, cp in enumerate(copies(r)):
            cp.start(priority=k)
        return c

    def wait(r, c):
        for cp in copies(r):
            cp.wait()
        return c

    lax.fori_loop(0, tm, start, 0)
    lax.fori_loop(0, tm, wait, 0)
    g1 = info_ref[:, 4:5]
    g2 = info_ref[:, 5:6]
    for s in range(slab):
        cols = slice(s * LANES, (s + 1) * LANES)
        rows = pl.ds(s, tm, stride=slab)
        mix = buf1[rows, :] * g1 + buf2[rows, :] * g2
        o_ref[:, cols] = x_ref[:, cols] + gate_ref[0][:, cols] * mix


def _moe_combine(tok, x, mods, mod_base, info, y_slab, pos, tm):
    t, d = x.shape
    slab = d // LANES
    grid_spec = pltpu.PrefetchScalarGridSpec(
        num_scalar_prefetch=1, grid=(t // tm,),
        in_specs=[pl.BlockSpec((tm, d), lambda i, p: (i, 0)),
                  pl.BlockSpec((1, 1, d), lambda i, p: (mod_base + tok.group_of(i * tm) * 6 + 5, 0, 0)),
                  pl.BlockSpec((tm, 128), lambda i, p: (i, 0)),
                  pl.BlockSpec(memory_space=pl.ANY)],
        out_specs=pl.BlockSpec((tm, d), lambda i, p: (i, 0)),
        scratch_shapes=[pltpu.VMEM((tm * slab, LANES), F32), pltpu.VMEM((tm * slab, LANES), F32),
                        pltpu.SemaphoreType.DMA(())])
    return pl.pallas_call(
        functools.partial(_combine_body, tm=tm), grid_spec=grid_spec,
        out_shape=jax.ShapeDtypeStruct((t, d), F32),
        compiler_params=_cparams(1), name="moe_combine",
    )(pos, x, mods, info, y_slab)


def _moe_ffn(tok, x, mods, mod_base, gnorm, lyr, w_router, b_router, w_gate, w_up, w_down):
    t, d = x.shape
    _, n_exp, _, dexp = w_gate.shape
    tm = tok.row_tile()
    h_slab, info, counts = _moe_route(tok, x, mods, mod_base, gnorm, w_router, b_router, tm)
    te = _pick(2 * t, (1024, 512, 256, 128, 64, 32, 16, 8))
    counts = counts[0, :n_exp].astype(jnp.int32)
    padded = (counts + te - 1) // te * te
    pad_end = jnp.cumsum(padded)
    pad_start = pad_end - padded
    idx = info[:, 0:2].astype(jnp.int32)
    rank = info[:, 2:4].astype(jnp.int32)
    pos = (pad_start[idx] + rank).reshape(-1)
    n_tiles = 2 * t // te + n_exp
    n_rows = n_tiles * te
    used = pad_end[-1] // te
    tile_row = jnp.minimum(jnp.arange(n_tiles, dtype=jnp.int32), used - 1)
    tile_exp = jnp.minimum(jnp.searchsorted(pad_end, tile_row * te, side="right"), n_exp - 1).astype(jnp.int32)
    tmc = _pick(tok.row_tile(), (256, 128, 64, 32, 16, 8))
    xs = _moe_dispatch(h_slab, pos, n_rows, d // LANES, tmc)
    hmid = _matmul(xs, w_gate, w2=w_up, w_lead=(lyr,), tm=te, tn=_pick(dexp, (512, 256, 128)),
                   tiles=(tile_exp, tile_row), slab_in=True, n_rows=n_rows, out_dtype=BF16, name="moe_up")
    y_slab = _matmul(hmid, w_down, w_lead=(lyr,), tm=te, tn=_pick(d, (512, 256, 128)), tk=_k_tile(dexp, 2048),
                     tiles=(tile_exp, tile_row), slab_out=True, x_resident=True, n_rows=n_rows, name="moe_down")
    return _moe_combine(tok, x, mods, mod_base, info, y_slab, pos, tmc)


def kernel(x_prompt, x_sample, c, cache_mla_ckv, cache_mla_kpe, state_ssd_fwd, state_ssd_bwd, c_ctx, norm1_g, norm2_g, w_mod, b_mod, norm_f_g, hy_w_in, hy_conv_w, hy_conv_b, hy_f_w1, hy_f_b1, hy_f_w2, hy_f_b2, hy_f_w3, hy_decay, hy_skip, hy_w_out, ssd_w_in, ssd_conv_w, ssd_conv_b, ssd_dt_bias, ssd_a_log, ssd_d, ssd_norm_g, ssd_w_out, mla_w_dq, mla_q_norm_g, mla_w_uq, mla_w_dkv, mla_kv_norm_g, mla_w_ukv, mla_w_o, ffn_w_gate, ffn_w_up, ffn_w_down, moe_w_router, moe_b_router, moe_w_gate, moe_w_up, moe_w_down):
    bp, lp, d = x_prompt.shape
    bs, ls, _ = x_sample.shape
    depth = w_mod.shape[0]
    tok = _Tokens(bp, lp, bs, ls)
    x = jnp.concatenate([x_prompt.reshape(tok.tp, d), x_sample.reshape(tok.ts, d)], axis=0)

    gp = -(-tok.groups // 8) * 8
    cond = jnp.concatenate([c_ctx[None, :], c, jnp.zeros((gp - tok.groups, d), F32)], axis=0)
    mods = _modulation(cond, w_mod, b_mod).reshape(depth * gp * 6, 1, d)

    ckv_new, kpe_new, sf_new, sb_new = [], [], [], []
    for i in range(depth):
        base = i * gp * 6
        j, kind = i // N_MIXERS, i % N_MIXERS
        if kind == 0:
            x = _hyena(tok, x, mods, base, norm1_g[i], j, hy_w_in, hy_conv_w[j], hy_conv_b[j], hy_f_w1[j],
                       hy_f_b1[j], hy_f_w2[j], hy_f_b2[j], hy_f_w3[j], hy_decay[j], hy_skip[j], hy_w_out)
        elif kind == 1:
            x, (s_f, s_b) = _ssd(tok, x, mods, base, norm1_g[i], ssd_w_in[j], ssd_conv_w[j], ssd_conv_b[j],
                                 ssd_dt_bias[j], ssd_a_log[j], ssd_d[j], ssd_norm_g[j], ssd_w_out[j],
                                 state_ssd_fwd[:, j], state_ssd_bwd[:, j])
            sf_new.append(s_f)
            sb_new.append(s_b)
        else:
            x, ckv, kpe = _mla(tok, x, mods, base, norm1_g[i], mla_w_dq[j], mla_q_norm_g[j], mla_w_uq[j],
                               mla_w_dkv[j], mla_kv_norm_g[j], mla_w_ukv[j], mla_w_o[j],
                               cache_mla_ckv[:, j], cache_mla_kpe[:, j])
            ckv_new.append(ckv)
            kpe_new.append(kpe)
        k = i // 2
        if i % 2 == 0:
            x = _dense_ffn(tok, x, mods, base, norm2_g[i], k, ffn_w_gate, ffn_w_up, ffn_w_down)
        else:
            x = _moe_ffn(tok, x, mods, base, norm2_g[i], k, moe_w_router[k], moe_b_router[k], moe_w_gate,
                         moe_w_up, moe_w_down)
    y = _rmsnorm(x, norm_f_g, tok.row_tile(512))
    y_prompt = y[:tok.tp].reshape(bp, lp, d)
    y_sample = y[tok.tp:].reshape(bs, ls, d)
    return (y_prompt, y_sample, jnp.stack(ckv_new, axis=1), jnp.stack(kpe_new, axis=1),
            jnp.stack(sf_new, axis=1), jnp.stack(sb_new, axis=1))
```
